```python
import math
import jax, jax.numpy as jnp
from jax import lax
import numpy as np

D_MODEL = 1024
BATCH = 16
SEQ = 2048
DEPTH = 4
DEC_BATCH = 8
DEC_SEQ = 4096
PAST_LEN = 128

GRID_W = 64
GROUP_W = D_MODEL // 4
FN_GROUPS = 4
FN_CH = GROUP_W // FN_GROUPS
DA_HEADS = 4
DA_VDIM = GROUP_W // DA_HEADS
DA_QDIM = DA_VDIM // 2
ROPE_THETA = 10000.0
Q_BLOCK = 128
NA_HEADS = 4
NA_HDIM = GROUP_W // NA_HEADS
NA_WIN_R = 8
NA_WIN_C = 16
NA_QCOLS = 16
NA_KCOLS = 32
SSM_HEADS = 4
SSM_HDIM = GROUP_W // SSM_HEADS
SSM_GROUPS = 2
SSM_HPG = SSM_HEADS // SSM_GROUPS
SSM_STATE = 128
SSM_CONV = 5
SSM_CHUNK = 128
SSM_XBC = GROUP_W + 2 * SSM_GROUPS * SSM_STATE
SSM_COLS = GROUP_W + SSM_XBC + 2 * SSM_HEADS
OFF_FN = 0
OFF_DA = OFF_FN + GROUP_W
OFF_NA = OFF_DA + 3 * GROUP_W
OFF_SSM = OFF_NA + 3 * GROUP_W
D_IN = OFF_SSM + SSM_COLS
D_FF = ((8 * D_MODEL + 3 * 256 - 1) // (3 * 256)) * 256
EPS = 1e-6

kernel_name = 'hybrid_bidir_encoder_fourier_diffattn_natten_ssd'


def _rms(x):
    xf = x.astype(jnp.float32)
    return xf * lax.rsqrt(jnp.mean(xf * xf, axis=-1, keepdims=True) + EPS)


def rmsnorm(x, g):
    return (_rms(x) * g.astype(jnp.float32)).astype(x.dtype)


def rope(x):
    l, d = x.shape[1], x.shape[-1]
    half = d // 2
    pos = jnp.arange(l, dtype=jnp.float32)
    inv = jnp.power(ROPE_THETA, -jnp.arange(half, dtype=jnp.float32) * (2.0 / d))
    ang = pos[:, None] * inv[None, :]
    shape = (1, l) + (1,) * (x.ndim - 3) + (half,)
    cos = jnp.cos(ang).reshape(shape)
    sin = jnp.sin(ang).reshape(shape)
    xf = x.astype(jnp.float32)
    x1, x2 = xf[..., :half], xf[..., half:]
    return jnp.concatenate([x1 * cos - x2 * sin, x2 * cos + x1 * sin], axis=-1).astype(x.dtype)


def fourier_mixer(u, w):
    b, l, _ = u.shape
    uf = u.astype(jnp.float32).reshape(b, l, FN_GROUPS, FN_CH)
    f = jnp.fft.fft2(uf, axes=(1, 3), norm='ortho').real
    return f.reshape(b, l, GROUP_W).astype(u.dtype) @ w


def diff_attention(u, lam_vecs, subln_g, lambda_init):
    b, l, _ = u.shape
    q, k, v = jnp.split(u, 3, axis=-1)
    q = rope(q.reshape(b, l, DA_HEADS, 2, DA_QDIM)) * (DA_QDIM ** -0.5)
    k = rope(k.reshape(b, l, DA_HEADS, 2, DA_QDIM))
    v = v.reshape(b, l, DA_HEADS, DA_VDIM)
    lv = lam_vecs.astype(jnp.float32)
    lam = jnp.exp(jnp.sum(lv[0] * lv[1])) - jnp.exp(jnp.sum(lv[2] * lv[3])) + lambda_init
    nb = l // Q_BLOCK
    qb = jnp.moveaxis(q.reshape(b, nb, Q_BLOCK, DA_HEADS, 2, DA_QDIM), 1, 0)

    def block(qi):
        s = jnp.einsum('bqhtd,bshtd->bhtqs', qi, k).astype(jnp.float32)
        p = jax.nn.softmax(s, axis=-1)
        a = p[:, :, 0] - lam * p[:, :, 1]
        return jnp.einsum('bhqs,bshd->bqhd', a.astype(v.dtype), v)

    o = lax.map(block, qb)
    o = jnp.moveaxis(o, 0, 1).reshape(b, l, DA_HEADS, DA_VDIM)
    o = rmsnorm(o, subln_g) * (1.0 - lambda_init)
    return o.reshape(b, l, GROUP_W)


def neighbourhood_attention(u, rpb):
    b, l, _ = u.shape
    rows = l // GRID_W
    wr = min(NA_WIN_R, rows)
    ncb = GRID_W // NA_QCOLS
    q, k, v = jnp.split(u, 3, axis=-1)
    q = q.reshape(b, rows, ncb, NA_QCOLS, NA_HEADS, NA_HDIM)
    k = k.reshape(b, rows, GRID_W, NA_HEADS, NA_HDIM)
    v = v.reshape(b, rows, GRID_W, NA_HEADS, NA_HDIM)
    r = jnp.arange(rows)
    row_idx = jnp.clip(r - wr // 2, 0, rows - wr)[:, None] + jnp.arange(wr)[None, :]
    c0 = jnp.arange(ncb) * NA_QCOLS
    col_idx = jnp.clip(c0 - NA_WIN_C // 2, 0, GRID_W - NA_KCOLS)[:, None] + jnp.arange(NA_KCOLS)[None, :]
    ri = row_idx[:, None, :, None]
    ci = col_idx[None, :, None, :]
    kg = k[:, ri, ci]
    vg = v[:, ri, ci]
    s = jnp.einsum('brcqhd,brcijhd->brchqij', q, kg).astype(jnp.float32) * (NA_HDIM ** -0.5)
    qc = c0[:, None] + jnp.arange(NA_QCOLS)[None, :]
    sc = jnp.clip(qc - NA_WIN_C // 2, 0, GRID_W - NA_WIN_C)
    kc = col_idx[:, None, :]
    valid = (kc >= sc[:, :, None]) & (kc < sc[:, :, None] + NA_WIN_C)
    dr = row_idx - r[:, None] + (NA_WIN_R - 1)
    dc = jnp.clip(kc - qc[:, :, None] + (NA_WIN_C - 1), 0, 2 * NA_WIN_C - 2)
    bias = rpb[:, dr[:, None, None, :, None], dc[None, :, :, None, :]]
    bias = jnp.moveaxis(bias, 0, 2).astype(jnp.float32)
    s = jnp.where(valid[:, None, :, None, :], s + bias, -jnp.inf)
    p = jax.nn.softmax(s, axis=(-2, -1))
    o = jnp.einsum('brchqij,brcijhd->brcqhd', p.astype(vg.dtype), vg)
    return o.reshape(b, l, GROUP_W)


def dwconv(x, w, bias):
    c = x.shape[-1]
    y = lax.conv_general_dilated(x, w[:, None, :].astype(x.dtype), (1,),
                                 [(SSM_CONV // 2, SSM_CONV // 2)],
                                 dimension_numbers=('NWC', 'WIO', 'NWC'),
                                 feature_group_count=c)
    return y + bias.astype(x.dtype)


def ssd_scan(x, dt, a, bm, cm):
    b, l = x.shape[:2]
    nc = l // SSM_CHUNK
    f32 = jnp.float32
    x = x.astype(f32).reshape(b, nc, SSM_CHUNK, SSM_GROUPS, SSM_HPG, SSM_HDIM)
    dt = dt.reshape(b, nc, SSM_CHUNK, SSM_GROUPS, SSM_HPG)
    bm = bm.astype(f32).reshape(b, nc, SSM_CHUNK, SSM_GROUPS, SSM_STATE)
    cm = cm.astype(f32).reshape(b, nc, SSM_CHUNK, SSM_GROUPS, SSM_STATE)
    acs = jnp.cumsum(dt * a.reshape(SSM_GROUPS, SSM_HPG), axis=2)
    xdt = x * dt[..., None]
    causal = jnp.tril(jnp.ones((SSM_CHUNK, SSM_CHUNK), dtype=bool))[:, :, None, None]
    seg = acs[:, :, :, None] - acs[:, :, None, :]
    decay_ls = jnp.exp(jnp.where(causal, seg, -jnp.inf))
    cb = jnp.einsum('bclgn,bcsgn->bclsg', cm, bm)
    y_diag = jnp.einsum('bclsgk,bcsgkp->bclgkp', cb[..., None] * decay_ls, xdt)
    decay_s = jnp.exp(acs[:, :, -1:] - acs)
    states = jnp.einsum('bcsgn,bcsgk,bcsgkp->bcgkpn', bm, decay_s, xdt)
    chunk_decay = jnp.exp(acs[:, :, -1])

    def step(h, inp):
        s_c, d_c = inp
        return h * d_c[..., None, None] + s_c, h

    h0 = jnp.zeros((b, SSM_GROUPS, SSM_HPG, SSM_HDIM, SSM_STATE), f32)
    _, prev = lax.scan(step, h0, (jnp.moveaxis(states, 1, 0), jnp.moveaxis(chunk_decay, 1, 0)))
    prev = jnp.moveaxis(prev, 0, 1)
    y_off = jnp.einsum('bclgn,bcgkpn->bclgkp', cm, prev) * jnp.exp(acs)[..., None]
    return (y_diag + y_off).reshape(b, l, SSM_GROUPS, SSM_HPG, SSM_HDIM)


def ssd_mixer(u, conv_w, conv_b, dt_bias, a_log, d_skip, norm_g):
    b, l, _ = u.shape
    z = u[..., :GROUP_W]
    xbc = u[..., GROUP_W:GROUP_W + SSM_XBC]
    dt_raw = u[..., GROUP_W + SSM_XBC:].reshape(b, l, 2, SSM_HEADS)
    xbc = jax.nn.silu(dwconv(xbc, conv_w, conv_b))
    x = xbc[..., :GROUP_W].reshape(b, l, SSM_GROUPS, SSM_HPG, SSM_HDIM)
    nbc = SSM_GROUPS * SSM_STATE
    bm = xbc[..., GROUP_W:GROUP_W + nbc].reshape(b, l, SSM_GROUPS, SSM_STATE)
    cm = xbc[..., GROUP_W + nbc:].reshape(b, l, SSM_GROUPS, SSM_STATE)
    dt = jax.nn.softplus(dt_raw.astype(jnp.float32) + dt_bias.astype(jnp.float32))
    a = -jnp.exp(a_log.astype(jnp.float32))
    flip = lambda t: jnp.flip(t, axis=1)
    y_f = ssd_scan(x, dt[:, :, 0], a[0], bm, cm)
    y_b = flip(ssd_scan(flip(x), flip(dt[:, :, 1]), a[1], flip(bm), flip(cm)))
    y = y_f + y_b + x.astype(jnp.float32) * d_skip.astype(jnp.float32).reshape(SSM_GROUPS, SSM_HPG, 1)
    y = y.reshape(b, l, GROUP_W) * jax.nn.silu(z.astype(jnp.float32))
    y = _rms(y.reshape(b, l, SSM_GROUPS, GROUP_W // SSM_GROUPS)).reshape(b, l, GROUP_W)
    return (y * norm_g.astype(jnp.float32)).astype(u.dtype)


def trunk(x, attn_norm_g, w_in, w_fourier, diff_lambda, diff_subln_g, na_rpb,
          ssm_conv_w, ssm_conv_b, ssm_dt_bias, ssm_A_log, ssm_D, ssm_norm_g,
          w_out, ffn_norm_g, w_gate, w_up, w_down, final_norm_g):
    for i in range(DEPTH):
        lambda_init = 0.8 - 0.6 * math.exp(-0.3 * i)
        h = rmsnorm(x, attn_norm_g[i])
        proj = h @ w_in[i]
        o_fn = fourier_mixer(proj[..., OFF_FN:OFF_DA], w_fourier[i])
        o_da = diff_attention(proj[..., OFF_DA:OFF_NA], diff_lambda[i], diff_subln_g[i], lambda_init)
        o_na = neighbourhood_attention(proj[..., OFF_NA:OFF_SSM], na_rpb[i])
        o_ssm = ssd_mixer(proj[..., OFF_SSM:], ssm_conv_w[i], ssm_conv_b[i], ssm_dt_bias[i],
                          ssm_A_log[i], ssm_D[i], ssm_norm_g[i])
        x = x + jnp.concatenate([o_fn, o_da, o_na, o_ssm], axis=-1) @ w_out[i]
        h = rmsnorm(x, ffn_norm_g[i])
        x = x + (jax.nn.silu(h @ w_gate[i]) * (h @ w_up[i])) @ w_down[i]
    return rmsnorm(x, final_norm_g)


def setup_inputs(seed: int = 0) -> dict:
    key = jax.random.key(seed)
    ks = jax.random.split(key, 24)
    f32 = jnp.float32
    nrm = lambda k, shape, scale: jax.random.normal(k, shape, f32) * scale
    dt0 = jnp.exp(jax.random.uniform(ks[10], (DEPTH, 2, SSM_HEADS), f32,
                                     math.log(1e-3), math.log(1e-1)))
    return {
        'x_prompt': nrm(ks[0], (BATCH, SEQ, D_MODEL), 1.0),
        'x_sample': nrm(ks[1], (DEC_BATCH, DEC_SEQ, D_MODEL), 1.0),
        'attn_norm_g': 1.0 + nrm(ks[2], (DEPTH, D_MODEL), 0.05),
        'w_in': nrm(ks[3], (DEPTH, D_MODEL, D_IN), D_MODEL ** -0.5),
        'w_fourier': nrm(ks[4], (DEPTH, GROUP_W, GROUP_W), GROUP_W ** -0.5),
        'diff_lambda': nrm(ks[5], (DEPTH, 4, DA_QDIM), 0.1),
        'diff_subln_g': 1.0 + nrm(ks[6], (DEPTH, DA_VDIM), 0.05),
        'na_rpb': nrm(ks[7], (DEPTH, NA_HEADS, 2 * NA_WIN_R - 1, 2 * NA_WIN_C - 1), 0.02),
        'ssm_conv_w': nrm(ks[8], (DEPTH, SSM_CONV, SSM_XBC), SSM_CONV ** -0.5),
        'ssm_conv_b': nrm(ks[9], (DEPTH, SSM_XBC), 0.01),
        'ssm_dt_bias': dt0 + jnp.log(-jnp.expm1(-dt0)),
        'ssm_A_log': jnp.log(jax.random.uniform(ks[11], (DEPTH, 2, SSM_HEADS), f32, 1.0, 16.0)),
        'ssm_D': 1.0 + nrm(ks[12], (DEPTH, SSM_HEADS), 0.05),
        'ssm_norm_g': 1.0 + nrm(ks[13], (DEPTH, GROUP_W), 0.05),
        'w_out': nrm(ks[14], (DEPTH, D_MODEL, D_MODEL), D_MODEL ** -0.5),
        'ffn_norm_g': 1.0 + nrm(ks[15], (DEPTH, D_MODEL), 0.05),
        'w_gate': nrm(ks[16], (DEPTH, D_MODEL, D_FF), D_MODEL ** -0.5),
        'w_up': nrm(ks[17], (DEPTH, D_MODEL, D_FF), D_MODEL ** -0.5),
        'w_down': nrm(ks[18], (DEPTH, D_FF, D_MODEL), D_FF ** -0.5),
        'final_norm_g': 1.0 + nrm(ks[19], (D_MODEL,), 0.05),
    }


def reference(x_prompt, x_sample, attn_norm_g, w_in, w_fourier, diff_lambda, diff_subln_g, na_rpb,
              ssm_conv_w, ssm_conv_b, ssm_dt_bias, ssm_A_log, ssm_D, ssm_norm_g,
              w_out, ffn_norm_g, w_gate, w_up, w_down, final_norm_g):
    y_prompt = trunk(x_prompt, attn_norm_g, w_in, w_fourier, diff_lambda, diff_subln_g, na_rpb,
                     ssm_conv_w, ssm_conv_b, ssm_dt_bias, ssm_A_log, ssm_D, ssm_norm_g,
                     w_out, ffn_norm_g, w_gate, w_up, w_down, final_norm_g)
    y_sample = trunk(x_sample, attn_norm_g, w_in, w_fourier, diff_lambda, diff_subln_g, na_rpb,
                     ssm_conv_w, ssm_conv_b, ssm_dt_bias, ssm_A_log, ssm_D, ssm_norm_g,
                     w_out, ffn_norm_g, w_gate, w_up, w_down, final_norm_g)
    return (y_prompt, y_sample)
```

```python
import functools
import math

import numpy as np
import jax
import jax.numpy as jnp
from jax import lax
from jax.experimental import pallas as pl
from jax.experimental.pallas import tpu as pltpu

F32 = jnp.float32
BF16 = jnp.bfloat16

D_MODEL = 1024
DEPTH = 4
GRID_W = 64
GROUP_W = D_MODEL // 4
FN_CH = 64
DA_QDIM = 32
ROPE_THETA = 10000.0
NA_HDIM = 64
NA_WIN_R = 8
NA_WIN_C = 16
SSM_HEADS = 4
SSM_CONV = 5
SSM_CHUNK = 128
SSM_XBC = 768
D_IN = 2824
D_IN_PAD = 2944
D_FF = 2816
EPS = 1e-6
NEG = -1e30

V7X_VMEM_LIMIT = 56 * 1024 * 1024


def _cparams(sem):
    return pltpu.CompilerParams(dimension_semantics=sem, vmem_limit_bytes=V7X_VMEM_LIMIT)


def _const_spec(shape):
    nd = len(shape)
    return pl.BlockSpec(shape, lambda *_: (0,) * nd, pipeline_mode=pl.Buffered(1))


def _rms_rows(x):
    return x * lax.rsqrt(jnp.mean(x * x, axis=-1, keepdims=True) + EPS)


def _seg_pos(i, segs, tile):
    pos, per = None, None
    for row0, bsz, seq in reversed(segs):
        t0, n = row0 // tile, seq // tile
        p = lax.rem(i - t0, n)
        if pos is None:
            pos, per = p, jnp.full((), n, jnp.int32)
        else:
            end = (row0 + bsz * seq) // tile
            pos = jnp.where(i < end, p, pos)
            per = jnp.where(i < end, n, per)
    return pos, per


def _in_proj_body(x_ref, g_ref, w_ref, cdft_ref, cos_ref, sa_ref, sb_ref,
                  xcs_ref, daq_ref, dak_ref, dav_ref, naq_ref, nak_ref, nav_ref, z_ref, xbc_ref, dt_ref):
    h = (_rms_rows(x_ref[...]) * g_ref[...]).astype(BF16)

    def proj(a, b):
        return jnp.dot(h, w_ref[:, a:b], preferred_element_type=F32)

    fn = proj(0, 256)
    xcs_ref[...] = jnp.dot(fn.astype(BF16), cdft_ref[...], preferred_element_type=F32).astype(BF16)

    cos, sa, sb = cos_ref[...], sa_ref[...], sb_ref[...]

    def rope(u):
        return u * cos + pltpu.roll(u, 112, 1) * sa + pltpu.roll(u, 16, 1) * sb

    scale = DA_QDIM ** -0.5
    for half in range(2):
        c0 = 256 + 128 * half
        daq_ref[:, 128 * half:128 * (half + 1)] = (rope(proj(c0, c0 + 128)) * scale).astype(BF16)
        dak_ref[:, 128 * half:128 * (half + 1)] = rope(proj(c0 + 256, c0 + 384)).astype(BF16)
    dav_ref[...] = proj(768, 1024).astype(BF16)
    naq_ref[...] = proj(1024, 1280).astype(BF16)
    nak_ref[...] = proj(1280, 1536).astype(BF16)
    nav_ref[...] = proj(1536, 1792).astype(BF16)
    z_ref[...] = proj(1792, 2048).astype(BF16)
    xbc_ref[...] = proj(2048, 2816)
    dt_ref[...] = proj(2816, 2944)


def _in_proj(x, g, w, cdft, rope_tabs, segs, tm):
    T = x.shape[0]
    cos, sa, sb = rope_tabs

    def tab_map(i):
        pos, _ = _seg_pos(i, segs, tm)
        return (pos, 0)

    row = lambda n: pl.BlockSpec((tm, n), lambda i: (i, 0))
    tab = pl.BlockSpec((tm, 128), tab_map)
    outs = [(512, BF16)] + [(256, BF16)] * 7 + [(SSM_XBC, F32), (128, F32)]
    return pl.pallas_call(
        _in_proj_body,
        grid=(T // tm,),
        in_specs=[row(D_MODEL), _const_spec((1, D_MODEL)), _const_spec((D_MODEL, D_IN_PAD)),
                  _const_spec((256, 512)), tab, tab, tab],
        out_specs=[row(n) for n, _ in outs],
        out_shape=[jax.ShapeDtypeStruct((T, n), dt) for n, dt in outs],
        compiler_params=_cparams(("parallel",)),
        name="in_proj",
    )(x, g, w, cdft, cos, sa, sb)


def _fourier_body(tab_ref, xcs_ref, w_ref, o_ref, *, L):
    acc = jnp.dot(tab_ref[:, :L], xcs_ref[:, :256], preferred_element_type=F32)
    acc += jnp.dot(tab_ref[:, L:], xcs_ref[:, 256:], preferred_element_type=F32)
    o_ref[...] = jnp.dot(acc.astype(BF16), w_ref[...], preferred_element_type=F32).astype(BF16)


def _fourier(xcs, tab, w, o_prev, seg, tm):
    row0, bsz, L = seg
    T = xcs.shape[0]
    blk0, mt = row0 // L, L // tm
    args = [tab, xcs, w]
    in_specs = [pl.BlockSpec((tm, 2 * L), lambda m, b: (m, 0)),
                pl.BlockSpec((L, 512), lambda m, b: (blk0 + b, 0)),
                _const_spec((256, 256))]
    aliases = {}
    if o_prev is not None:
        args.append(o_prev)
        in_specs.append(pl.BlockSpec(memory_space=pl.ANY))
        aliases = {3: 0}

    def body(*refs):
        _fourier_body(refs[0], refs[1], refs[2], refs[-1], L=L)

    return pl.pallas_call(
        body,
        grid=(mt, bsz),
        in_specs=in_specs,
        out_specs=pl.BlockSpec((tm, 256), lambda m, b: ((blk0 + b) * mt + m, 0)),
        out_shape=jax.ShapeDtypeStruct((T, 256), BF16),
        input_output_aliases=aliases,
        compiler_params=_cparams(("parallel", "parallel")),
        name=f"fourier_{L}",
    )(*args)


def _group_mean_sq(o, gmat_ref):
    sq = o * o
    hi = sq.astype(BF16)
    lo = (sq - hi.astype(F32)).astype(BF16)
    g = gmat_ref[...]
    return jnp.dot(hi, g, preferred_element_type=F32) + jnp.dot(lo, g, preferred_element_type=F32)


def _da_body(lam_ref, g_ref, gmat_ref, q_ref, k_ref, v_ref, o_ref, *, lambda_init):
    lv = lam_ref[...]
    l1 = jnp.sum(lv[0:1] * lv[1:2], axis=-1, keepdims=True)
    l2 = jnp.sum(lv[2:3] * lv[3:4], axis=-1, keepdims=True)
    lam = jnp.exp(l1) - jnp.exp(l2) + lambda_init
    q, k, v = q_ref[...], k_ref[...], v_ref[...]
    lane = lax.broadcasted_iota(jnp.int32, q.shape, 1)
    heads = []
    for j in range(2):
        probs = []
        for t in range(2):
            lo = 64 * j + 32 * t
            qm = jnp.where((lane >= lo) & (lane < lo + 32), q, jnp.zeros_like(q))
            s = lax.dot_general(qm, k, (((1,), (1,)), ((), ())), preferred_element_type=F32)
            e = jnp.exp(s - jnp.max(s, axis=-1, keepdims=True))
            probs.append(e * (1.0 / jnp.sum(e, axis=-1, keepdims=True)))
        a = (probs[0] - lam * probs[1]).astype(BF16)
        heads.append(jnp.dot(a, v, preferred_element_type=F32))
    o = jnp.where(lane < 64, heads[0], heads[1])
    o = o * lax.rsqrt(_group_mean_sq(o, gmat_ref) + EPS) * g_ref[...]
    o_ref[...] = (o * (1.0 - lambda_init)).astype(BF16)


def _diff_attn(q, k, v, lam_vecs, g128, gmat, o_prev, seg, tq, lambda_init):
    row0, bsz, L = seg
    T = q.shape[0]
    blk0, nq = row0 // L, L // tq
    kv = pl.BlockSpec((L, 128), lambda b, p, i: (blk0 + b, p))
    qo = pl.BlockSpec((tq, 128), lambda b, p, i: ((blk0 + b) * nq + i, p))
    args = [lam_vecs, g128, gmat, q, k, v]
    in_specs = [_const_spec((4, DA_QDIM)), _const_spec((1, 128)), _const_spec((128, 128)), qo, kv, kv]
    aliases = {}
    if o_prev is not None:
        args.append(o_prev)
        in_specs.append(pl.BlockSpec(memory_space=pl.ANY))
        aliases = {6: 0}

    def body(*refs):
        _da_body(*refs[:6], refs[-1], lambda_init=lambda_init)

    return pl.pallas_call(
        body,
        grid=(bsz, 2, nq),
        in_specs=in_specs,
        out_specs=qo,
        out_shape=jax.ShapeDtypeStruct((T, 256), BF16),
        input_output_aliases=aliases,
        compiler_params=_cparams(("parallel", "parallel", "parallel")),
        name=f"diff_attn_{L}",
    )(*args)


def _na_bias_table(rpb):
    d = np.arange(8)[:, None, None, None]
    i = np.arange(8)[None, :, None, None]
    qc = np.arange(GRID_W)[None, None, :, None]
    kc = np.arange(GRID_W)[None, None, None, :]
    dr = np.broadcast_to(i - d + (NA_WIN_R - 1), (8, 8, GRID_W, GRID_W))
    dc = np.broadcast_to(np.clip(kc - qc + (NA_WIN_C - 1), 0, 2 * NA_WIN_C - 2), (8, 8, GRID_W, GRID_W))
    sc = np.clip(qc - NA_WIN_C // 2, 0, GRID_W - NA_WIN_C)
    valid = np.broadcast_to((kc >= sc) & (kc < sc + NA_WIN_C), (8, 8, GRID_W, GRID_W))
    bias = rpb.astype(F32)[:, dr, dc]
    bias = jnp.where(valid[None], bias, NEG)
    bias = jnp.transpose(bias, (1, 0, 3, 2, 4))
    return bias.reshape(8, 2, 2 * GRID_W, 8 * GRID_W)


def _na_body(bias_ref, q_ref, k_ref, v_ref, o_ref, *, rows):
    lane = lax.broadcasted_iota(jnp.int32, (GRID_W, 128), 1)
    scale = NA_HDIM ** -0.5

    def one_row(r, carry):
        s0 = jnp.clip(r - NA_WIN_R // 2, 0, rows - NA_WIN_R)
        q = q_ref[pl.ds(pl.multiple_of(r * GRID_W, GRID_W), GRID_W), :]
        zero = jnp.zeros_like(q)
        qs = jnp.concatenate([jnp.where(lane < 64, q, zero), jnp.where(lane >= 64, q, zero)], axis=0)
        win = pl.ds(pl.multiple_of(s0 * GRID_W, GRID_W), NA_WIN_R * GRID_W)
        s = lax.dot_general(qs, k_ref[win, :], (((1,), (1,)), ((), ())), preferred_element_type=F32)
        s = s * scale + bias_ref[r - s0]
        e = jnp.exp(s - jnp.max(s, axis=-1, keepdims=True))
        p = (e * (1.0 / jnp.sum(e, axis=-1, keepdims=True))).astype(BF16)
        o2 = jnp.dot(p, v_ref[win, :], preferred_element_type=F32)
        o = jnp.where(lane < 64, o2[:GRID_W], o2[GRID_W:])
        o_ref[pl.ds(pl.multiple_of(r * GRID_W, GRID_W), GRID_W), :] = o.astype(BF16)
        return carry

    lax.fori_loop(0, rows, one_row, 0)


def _nbr_attn(q, k, v, bias, o_prev, seg):
    row0, bsz, L = seg
    T = q.shape[0]
    blk0, rows = row0 // L, L // GRID_W
    assert rows >= NA_WIN_R
    blk = pl.BlockSpec((L, 128), lambda b, p: (blk0 + b, p))
    args = [bias, q, k, v]
    in_specs = [pl.BlockSpec((8, None, 128, 512), lambda b, p: (0, p, 0, 0)), blk, blk, blk]
    aliases = {}
    if o_prev is not None:
        args.append(o_prev)
        in_specs.append(pl.BlockSpec(memory_space=pl.ANY))
        aliases = {4: 0}

    def body(*refs):
        _na_body(*refs[:4], refs[-1], rows=rows)

    return pl.pallas_call(
        body,
        grid=(bsz, 2),
        in_specs=in_specs,
        out_specs=blk,
        out_shape=jax.ShapeDtypeStruct((T, 256), BF16),
        input_output_aliases=aliases,
        compiler_params=_cparams(("parallel", "parallel")),
        name=f"nbr_attn_{L}",
    )(*args)


CONV_HALO = 8


def _conv_body(w_ref, b_ref, prev_ref, cur_ref, next_ref, o_ref, buf_ref, *, segs, tc):
    i = pl.program_id(0)
    pos, per = _seg_pos(i, segs, tc)
    buf_ref[0:CONV_HALO, :] = prev_ref[...]
    buf_ref[CONV_HALO:CONV_HALO + tc, :] = cur_ref[...]
    buf_ref[CONV_HALO + tc:, :] = next_ref[...]

    @pl.when(pos == 0)
    def _():
        buf_ref[0:CONV_HALO, :] = jnp.zeros((CONV_HALO, SSM_XBC), F32)

    @pl.when(pos == per - 1)
    def _():
        buf_ref[CONV_HALO + tc:, :] = jnp.zeros((CONV_HALO, SSM_XBC), F32)

    rc = 64
    for r0 in range(0, tc, rc):
        acc = jnp.broadcast_to(b_ref[...], (rc, SSM_XBC))
        for j in range(SSM_CONV):
            off = CONV_HALO - SSM_CONV // 2 + j + r0
            acc = acc + buf_ref[off:off + rc, :] * w_ref[j:j + 1, :]
        o_ref[r0:r0 + rc, :] = (acc * jax.nn.sigmoid(acc)).astype(BF16)


def _conv_silu(xbc, w8, b, segs, tc):
    T = xbc.shape[0]
    hb = tc // CONV_HALO
    last = T // CONV_HALO - 1
    return pl.pallas_call(
        functools.partial(_conv_body, segs=segs, tc=tc),
        grid=(T // tc,),
        in_specs=[_const_spec((8, SSM_XBC)), _const_spec((1, SSM_XBC)),
                  pl.BlockSpec((CONV_HALO, SSM_XBC), lambda i: (jnp.maximum(i * hb - 1, 0), 0)),
                  pl.BlockSpec((tc, SSM_XBC), lambda i: (i, 0)),
                  pl.BlockSpec((CONV_HALO, SSM_XBC), lambda i: (jnp.minimum((i + 1) * hb, last), 0))],
        out_specs=pl.BlockSpec((tc, SSM_XBC), lambda i: (i, 0)),
        out_shape=jax.ShapeDtypeStruct((T, SSM_XBC), BF16),
        scratch_shapes=[pltpu.VMEM((tc + 2 * CONV_HALO, SSM_XBC), F32)],
        compiler_params=_cparams(("parallel",)),
        name="ssd_conv",
    )(w8, b, xbc, xbc, xbc)


def _split3(x):
    hi = x.astype(BF16)
    r1 = x - hi.astype(F32)
    mid = r1.astype(BF16)
    lo = (r1 - mid.astype(F32)).astype(BF16)
    return hi, mid, lo


def _ssd_body(dtb_ref, alog_ref, dskip_ref, ng_ref, act_ref, z_ref, dt_ref, o_ref, y_ref, h_ref, *, L):
    C = SSM_CHUNK
    nc = L // C
    rix = lax.broadcasted_iota(jnp.int32, (C, C), 0)
    cix = lax.broadcasted_iota(jnp.int32, (C, C), 1)
    left = cix < 64
    dtb = dtb_ref[...]
    a = -jnp.exp(alog_ref[...])

    def pair(u, c0):
        return jnp.where(left[:u.shape[0]], u[:, c0:c0 + 1], u[:, c0 + 1:c0 + 2])

    def chunk(c, direction):
        fwd = direction == 0
        rows = pl.ds(pl.multiple_of(c * C, C), C)
        dt = jax.nn.softplus(dt_ref[rows, :] + dtb)
        keep = (cix <= rix) if fwd else (cix >= rix)
        tri = jnp.where(keep, 1.0, 0.0).astype(BF16)
        acs = sum(jnp.dot(tri, part, preferred_element_type=F32) for part in _split3(dt * a))
        acs_t = acs.T
        end = C - 1 if fwd else 0
        act = act_ref[rows, :]
        ys = []
        for g in range(2):
            c0 = 4 * direction + 2 * g
            xg = act[:, 128 * g:128 * (g + 1)].astype(F32)
            bg = act[:, 256 + 128 * g:256 + 128 * (g + 1)]
            cg = act[:, 512 + 128 * g:512 + 128 * (g + 1)]
            cb = lax.dot_general(cg, bg, (((1,), (1,)), ((), ())), preferred_element_type=F32)
            xdt = xg * pair(dt, c0)
            xdt16 = xdt.astype(BF16)
            yd = []
            for hh in range(2):
                seg = acs[:, c0 + hh:c0 + hh + 1] - acs_t[c0 + hh:c0 + hh + 1, :]
                m = cb * jnp.exp(jnp.where(keep, seg, NEG))
                yd.append(jnp.dot(m.astype(BF16), xdt16, preferred_element_type=F32))
            y_diag = jnp.where(left, yd[0], yd[1])
            acs_end = jnp.where(left[:1], acs_t[c0:c0 + 1, end:end + 1], acs_t[c0 + 1:c0 + 2, end:end + 1])
            acs_p = pair(acs, c0)
            w = (xdt * jnp.exp(acs_end - acs_p)).astype(BF16)
            st = jnp.dot(bg.astype(F32).T.astype(BF16), w, preferred_element_type=F32)
            h_prev = h_ref[g]
            y_off = jnp.dot(cg, h_prev.astype(BF16), preferred_element_type=F32) * jnp.exp(acs_p)
            h_ref[g] = h_prev * jnp.exp(acs_end) + st
            ys.append(y_diag + y_off)
        y = jnp.concatenate(ys, axis=1)
        if fwd:
            y_ref[rows, :] = y
        else:
            xs = act[:, 0:256].astype(F32)
            y = y + y_ref[rows, :] + xs * dskip_ref[...]
            zf = z_ref[rows, :].astype(F32)
            y = y * (zf * jax.nn.sigmoid(zf))
            y = jnp.concatenate([_rms_rows(y[:, :128]), _rms_rows(y[:, 128:])], axis=1)
            o_ref[rows, :] = (y * ng_ref[...]).astype(BF16)

    for direction in range(2):
        h_ref[...] = jnp.zeros_like(h_ref)

        def step(t, carry, direction=direction):
            chunk(t if direction == 0 else nc - 1 - t, direction)
            return carry

        lax.fori_loop(0, nc, step, 0)


def _ssd_scan(act, z, dt, dtb, alog, dskip, ng, o_prev, seg):
    row0, bsz, L = seg
    T = act.shape[0]
    blk0 = row0 // L
    blk = lambda n: pl.BlockSpec((L, n), lambda b: (blk0 + b, 0))
    args = [dtb, alog, dskip, ng, act, z, dt]
    in_specs = [_const_spec((1, 128)), _const_spec((1, 128)), _const_spec((1, 256)), _const_spec((1, 256)),
                blk(SSM_XBC), blk(256), blk(128)]
    aliases = {}
    if o_prev is not None:
        args.append(o_prev)
        in_specs.append(pl.BlockSpec(memory_space=pl.ANY))
        aliases = {7: 0}

    def body(*refs):
        n = len(args)
        _ssd_body(*refs[:7], *refs[n:], L=L)

    return pl.pallas_call(
        body,
        grid=(bsz,),
        in_specs=in_specs,
        out_specs=blk(256),
        out_shape=jax.ShapeDtypeStruct((T, 256), BF16),
        scratch_shapes=[pltpu.VMEM((L, 256), F32), pltpu.VMEM((2, 128, 128), F32)],
        input_output_aliases=aliases,
        compiler_params=_cparams(("parallel",)),
        name=f"ssd_scan_{L}",
    )(*args)


FF_CHUNK = 256


def _out_ffn_body(x_ref, ofn_ref, oda_ref, ona_ref, ossm_ref, wo_ref, g_ref, wg_ref, wu_ref, wd_ref, fg_ref,
                  o_ref, *, final):
    mix = jnp.concatenate([ofn_ref[...], oda_ref[...], ona_ref[...], ossm_ref[...]], axis=1)
    x = x_ref[...] + jnp.dot(mix, wo_ref[...], preferred_element_type=F32)
    h = (_rms_rows(x) * g_ref[...]).astype(BF16)
    gate = jnp.dot(h, wg_ref[...], preferred_element_type=F32)
    up = jnp.dot(h, wu_ref[...], preferred_element_type=F32)
    act = (gate * jax.nn.sigmoid(gate) * up).astype(BF16)
    x = x + jnp.dot(act, wd_ref[...], preferred_element_type=F32)
    if final:
        x = _rms_rows(x) * fg_ref[...]
    o_ref[...] = x


def _out_ffn(x, mixes, wo, g, wg, wu, wd, fg, tm, final):
    T = x.shape[0]
    row = lambda n: pl.BlockSpec((tm, n), lambda i: (i, 0))
    return pl.pallas_call(
        functools.partial(_out_ffn_body, final=final),
        grid=(T // tm,),
        in_specs=[row(D_MODEL)] + [row(256)] * 4 +
                 [_const_spec((D_MODEL, D_MODEL)), _const_spec((1, D_MODEL)), _const_spec((D_MODEL, D_FF)),
                  _const_spec((D_MODEL, D_FF)), _const_spec((D_FF, D_MODEL)), _const_spec((1, D_MODEL))],
        out_specs=row(D_MODEL),
        out_shape=jax.ShapeDtypeStruct((T, D_MODEL), F32),
        input_output_aliases={0: 0},
        compiler_params=_cparams(("parallel",)),
        name="out_ffn",
    )(x, *mixes, wo, g, wg, wu, wd, fg)


def _channel_dft():
    c = np.arange(GROUP_W)
    same = (c[:, None] // FN_CH) == (c[None, :] // FN_CH)
    ang = 2.0 * np.pi * ((c[:, None] % FN_CH) * (c[None, :] % FN_CH) % FN_CH) / FN_CH
    cs = np.concatenate([np.where(same, np.cos(ang), 0.0), np.where(same, np.sin(ang), 0.0)], axis=1)
    return jnp.asarray(cs / math.sqrt(FN_CH), BF16)


def _seq_dft(L):
    split = 64
    l = jnp.arange(L, dtype=jnp.int32)[None, :]
    ka = jnp.arange(L // split, dtype=jnp.int32)[:, None] * split
    kb = jnp.arange(split, dtype=jnp.int32)[:, None]
    ang = lambda kk: (2.0 * math.pi / L) * lax.rem(kk * l, L).astype(F32)
    ca, sa, cb, sb = jnp.cos(ang(ka)), jnp.sin(ang(ka)), jnp.cos(ang(kb)), jnp.sin(ang(kb))
    cc = ca[:, None, :] * cb[None, :, :] - sa[:, None, :] * sb[None, :, :]
    ss = sa[:, None, :] * cb[None, :, :] + ca[:, None, :] * sb[None, :, :]
    tab = jnp.concatenate([cc.reshape(L, L), -ss.reshape(L, L)], axis=1) * (L ** -0.5)
    return tab.astype(BF16)


def _rope_tables(max_len):
    half = DA_QDIM // 2
    pos = jnp.arange(max_len, dtype=F32)
    inv = jnp.power(ROPE_THETA, -jnp.arange(half, dtype=F32) * (2.0 / DA_QDIM))
    ang = pos[:, None] * inv[None, :]
    cos, sin = jnp.cos(ang), jnp.sin(ang)
    zero = jnp.zeros_like(sin)
    reps = 128 // DA_QDIM
    return (jnp.tile(jnp.concatenate([cos, cos], axis=1), (1, reps)),
            jnp.tile(jnp.concatenate([-sin, zero], axis=1), (1, reps)),
            jnp.tile(jnp.concatenate([zero, sin], axis=1), (1, reps)))


def _trunk(x, segs, p, tm_proj=512, tm_ffn=256, tm_dft=512, tq=128, tc=512):
    max_len = max(s[2] for s in segs)
    rope_tabs = _rope_tables(max_len)
    cdft = _channel_dft()
    dft_tabs = {L: _seq_dft(L) for L in sorted({s[2] for s in segs})}
    c = np.arange(128)
    gmat = jnp.asarray(((c[:, None] // 64) == (c[None, :] // 64)) / 64.0, BF16)
    row2 = lambda v: v.reshape(1, -1).astype(F32)
    pad128 = lambda v: jnp.pad(v.reshape(1, -1).astype(F32), ((0, 0), (0, 128 - v.size)))

    def over_segments(fn):
        out = None
        for seg in segs:
            out = fn(out, seg)
        return out

    for i in range(DEPTH):
        lambda_init = 0.8 - 0.6 * math.exp(-0.3 * i)
        w_in = jnp.pad(p['w_in'][i], ((0, 0), (0, D_IN_PAD - D_IN))).astype(BF16)
        (xcs, daq, dak, dav, naq, nak, nav, z, xbc, dt) = _in_proj(
            x, row2(p['attn_norm_g'][i]), w_in, cdft, rope_tabs, segs, tm_proj)

        wf = p['w_fourier'][i].astype(BF16)
        o_fn = over_segments(lambda o, seg: _fourier(xcs, dft_tabs[seg[2]], wf, o, seg, min(tm_dft, seg[2])))

        g128 = jnp.tile(row2(p['diff_subln_g'][i]), (1, 2))
        lam_vecs = p['diff_lambda'][i].astype(F32)
        o_da = over_segments(lambda o, seg: _diff_attn(daq, dak, dav, lam_vecs, g128, gmat, o, seg,
                                                       min(tq, seg[2]), lambda_init))

        bias = _na_bias_table(p['na_rpb'][i])
        o_na = over_segments(lambda o, seg: _nbr_attn(naq, nak, nav, bias, o, seg))

        w8 = jnp.pad(p['ssm_conv_w'][i].astype(F32), ((0, 8 - SSM_CONV), (0, 0)))
        act = _conv_silu(xbc, w8, row2(p['ssm_conv_b'][i]), segs, tc)
        dtb, alog = pad128(p['ssm_dt_bias'][i]), pad128(p['ssm_A_log'][i])
        dskip = jnp.repeat(row2(p['ssm_D'][i]), GROUP_W // SSM_HEADS, axis=1)
        ng = row2(p['ssm_norm_g'][i])
        o_ssm = over_segments(lambda o, seg: _ssd_scan(act, z, dt, dtb, alog, dskip, ng, o, seg))

        x = _out_ffn(x, (o_fn, o_da, o_na, o_ssm), p['w_out'][i].astype(BF16), row2(p['ffn_norm_g'][i]),
                     p['w_gate'][i].astype(BF16), p['w_up'][i].astype(BF16), p['w_down'][i].astype(BF16),
                     row2(p['final_norm_g']), tm_ffn, final=(i == DEPTH - 1))
    return x


def kernel(x_prompt, x_sample, attn_norm_g, w_in, w_fourier, diff_lambda, diff_subln_g, na_rpb, ssm_conv_w,
           ssm_conv_b, ssm_dt_bias, ssm_A_log, ssm_D, ssm_norm_g, w_out, ffn_norm_g, w_gate, w_up, w_down,
           final_norm_g):
    params = dict(attn_norm_g=attn_norm_g, w_in=w_in, w_fourier=w_fourier, diff_lambda=diff_lambda,
                  diff_subln_g=diff_subln_g, na_rpb=na_rpb, ssm_conv_w=ssm_conv_w, ssm_conv_b=ssm_conv_b,
                  ssm_dt_bias=ssm_dt_bias, ssm_A_log=ssm_A_log, ssm_D=ssm_D, ssm_norm_g=ssm_norm_g,
                  w_out=w_out, ffn_norm_g=ffn_norm_g, w_gate=w_gate, w_up=w_up, w_down=w_down,
                  final_norm_g=final_norm_g)
    bp, lp, _ = x_prompt.shape
    bs, ls, _ = x_sample.shape
    n_p = bp * lp
    segs = ((0, bp, lp), (n_p, bs, ls))
    x = jnp.concatenate([x_prompt.reshape(n_p, D_MODEL), x_sample.reshape(bs * ls, D_MODEL)], axis=0)
    y = _trunk(x, segs, params)
    return y[:n_p].reshape(x_prompt.shape), y[n_p:].reshape(x_sample.shape)
```

```python
import functools
import math

import numpy as np
import jax
import jax.numpy as jnp
from jax import lax
from jax.experimental import pallas as pl
from jax.experimental.pallas import tpu as pltpu

F32 = jnp.float32
BF16 = jnp.bfloat16

D_MODEL = 1024
DEPTH = 4
GRID_W = 64
GROUP_W = D_MODEL // 4
FN_CH = 64
DA_QDIM = 32
ROPE_THETA = 10000.0
NA_HDIM = 64
NA_WIN_R = 8
NA_WIN_C = 16
SSM_HEADS = 4
SSM_CONV = 5
SSM_CHUNK = 128
SSM_XBC = 768
D_IN = 2824
D_IN_PAD = 2944
D_FF = 2816
EPS = 1e-6
NEG = -1e30
LOG2E = math.log2(math.e)

V7X_VMEM_LIMIT = 56 * 1024 * 1024


def _cparams(sem):
    return pltpu.CompilerParams(dimension_semantics=sem, vmem_limit_bytes=V7X_VMEM_LIMIT)


def _const_spec(shape):
    nd = len(shape)
    return pl.BlockSpec(shape, lambda *_: (0,) * nd, pipeline_mode=pl.Buffered(1))


def _rms_rows(x):
    return x * lax.rsqrt(jnp.mean(x * x, axis=-1, keepdims=True) + EPS)


def _seg_pos(i, segs, tile):
    pos, per = None, None
    for row0, bsz, seq in reversed(segs):
        t0, n = row0 // tile, seq // tile
        p = lax.rem(i - t0, n)
        if pos is None:
            pos, per = p, jnp.full((), n, jnp.int32)
        else:
            end = (row0 + bsz * seq) // tile
            pos = jnp.where(i < end, p, pos)
            per = jnp.where(i < end, n, per)
    return pos, per


def _in_proj_body(x_ref, g_ref, w_ref, cdft_ref, cos_ref, sa_ref, sb_ref,
                  xcs_ref, daq_ref, dak_ref, dav_ref, naq_ref, nak_ref, nav_ref, z_ref, xbc_ref, dt_ref):
    h = (_rms_rows(x_ref[...]) * g_ref[...]).astype(BF16)

    def proj(a, b):
        return jnp.dot(h, w_ref[:, a:b], preferred_element_type=F32)

    fn = proj(0, 256)
    xcs_ref[...] = jnp.dot(fn.astype(BF16), cdft_ref[...], preferred_element_type=F32).astype(BF16)

    cos, sa, sb = cos_ref[...], sa_ref[...], sb_ref[...]

    def rope(u):
        return u * cos + pltpu.roll(u, 112, 1) * sa + pltpu.roll(u, 16, 1) * sb

    scale = DA_QDIM ** -0.5 * LOG2E
    for half in range(2):
        c0 = 256 + 128 * half
        daq_ref[:, 128 * half:128 * (half + 1)] = (rope(proj(c0, c0 + 128)) * scale).astype(BF16)
        dak_ref[:, 128 * half:128 * (half + 1)] = rope(proj(c0 + 256, c0 + 384)).astype(BF16)
    dav_ref[...] = proj(768, 1024).astype(BF16)
    naq_ref[...] = proj(1024, 1280).astype(BF16)
    nak_ref[...] = proj(1280, 1536).astype(BF16)
    nav_ref[...] = proj(1536, 1792).astype(BF16)
    z_ref[...] = proj(1792, 2048).astype(BF16)
    xbc_ref[...] = proj(2048, 2816)
    dt_ref[...] = proj(2816, 2944)


def _in_proj(x, g, w, cdft, rope_tabs, segs, tm):
    T = x.shape[0]
    cos, sa, sb = rope_tabs

    def tab_map(i):
        pos, _ = _seg_pos(i, segs, tm)
        return (pos, 0)

    row = lambda n: pl.BlockSpec((tm, n), lambda i: (i, 0))
    tab = pl.BlockSpec((tm, 128), tab_map)
    outs = [(512, BF16)] + [(256, BF16)] * 7 + [(SSM_XBC, F32), (128, F32)]
    return pl.pallas_call(
        _in_proj_body,
        grid=(T // tm,),
        in_specs=[row(D_MODEL), _const_spec((1, D_MODEL)), _const_spec((D_MODEL, D_IN_PAD)),
                  _const_spec((256, 512)), tab, tab, tab],
        out_specs=[row(n) for n, _ in outs],
        out_shape=[jax.ShapeDtypeStruct((T, n), dt) for n, dt in outs],
        compiler_params=_cparams(("parallel",)),
        name="in_proj",
    )(x, g, w, cdft, cos, sa, sb)


def _fourier_body(tab_ref, xcs_ref, w_ref, o_ref, *, L):
    acc = jnp.dot(tab_ref[:, :L], xcs_ref[:, :256], preferred_element_type=F32)
    acc += jnp.dot(tab_ref[:, L:], xcs_ref[:, 256:], preferred_element_type=F32)
    o_ref[...] = jnp.dot(acc.astype(BF16), w_ref[...], preferred_element_type=F32).astype(BF16)


def _fourier(xcs, tab, w, o_prev, seg, tm):
    row0, bsz, L = seg
    T = xcs.shape[0]
    blk0, mt = row0 // L, L // tm
    args = [tab, xcs, w]
    in_specs = [pl.BlockSpec((tm, 2 * L), lambda m, b: (m, 0)),
                pl.BlockSpec((L, 512), lambda m, b: (blk0 + b, 0)),
                _const_spec((256, 256))]
    aliases = {}
    if o_prev is not None:
        args.append(o_prev)
        in_specs.append(pl.BlockSpec(memory_space=pl.ANY))
        aliases = {3: 0}

    def body(*refs):
        _fourier_body(refs[0], refs[1], refs[2], refs[-1], L=L)

    return pl.pallas_call(
        body,
        grid=(mt, bsz),
        in_specs=in_specs,
        out_specs=pl.BlockSpec((tm, 256), lambda m, b: ((blk0 + b) * mt + m, 0)),
        out_shape=jax.ShapeDtypeStruct((T, 256), BF16),
        input_output_aliases=aliases,
        compiler_params=_cparams(("parallel", "parallel")),
        name=f"fourier_{L}",
    )(*args)


def _group_mean_sq(o, gmat_ref):
    sq = o * o
    hi = sq.astype(BF16)
    lo = (sq - hi.astype(F32)).astype(BF16)
    g = gmat_ref[...]
    return jnp.dot(hi, g, preferred_element_type=F32) + jnp.dot(lo, g, preferred_element_type=F32)


def _da_body(lam_ref, g_ref, gmat_ref, q_ref, k_ref, v_ref, o_ref, *, lambda_init):
    lv = lam_ref[...]
    l1 = jnp.sum(lv[0:1] * lv[1:2], axis=-1, keepdims=True)
    l2 = jnp.sum(lv[2:3] * lv[3:4], axis=-1, keepdims=True)
    lam = jnp.exp(l1) - jnp.exp(l2) + lambda_init
    q, k, v = q_ref[...], k_ref[...], v_ref[...]
    lane = lax.broadcasted_iota(jnp.int32, q.shape, 1)

    def scores(j):
        out = []
        for t in range(2):
            lo = 64 * j + 32 * t
            qm = jnp.where((lane >= lo) & (lane < lo + 32), q, jnp.zeros_like(q))
            out.append(lax.dot_general(qm, k, (((1,), (1,)), ((), ())), preferred_element_type=F32))
        return out

    def head(ss):
        es = [jnp.exp2(s - jnp.max(s, axis=-1, keepdims=True)) for s in ss]
        sums = [jnp.sum(e, axis=-1, keepdims=True) for e in es]
        a = (es[0] - (lam * sums[0] / sums[1]) * es[1]).astype(BF16)
        return jnp.dot(a, v, preferred_element_type=F32) * (1.0 / sums[0])

    s0, s1 = scores(0), scores(1)
    o = jnp.where(lane < 64, head(s0), head(s1))
    o = o * lax.rsqrt(_group_mean_sq(o, gmat_ref) + EPS) * g_ref[...]
    o_ref[...] = (o * (1.0 - lambda_init)).astype(BF16)


def _diff_attn(q, k, v, lam_vecs, g128, gmat, o_prev, seg, tq, lambda_init):
    row0, bsz, L = seg
    T = q.shape[0]
    blk0, nq = row0 // L, L // tq
    kv = pl.BlockSpec((L, 128), lambda b, p, i: (blk0 + b, p))
    qo = pl.BlockSpec((tq, 128), lambda b, p, i: ((blk0 + b) * nq + i, p))
    args = [lam_vecs, g128, gmat, q, k, v]
    in_specs = [_const_spec((4, DA_QDIM)), _const_spec((1, 128)), _const_spec((128, 128)), qo, kv, kv]
    aliases = {}
    if o_prev is not None:
        args.append(o_prev)
        in_specs.append(pl.BlockSpec(memory_space=pl.ANY))
        aliases = {6: 0}

    def body(*refs):
        _da_body(*refs[:6], refs[-1], lambda_init=lambda_init)

    return pl.pallas_call(
        body,
        grid=(bsz, 2, nq),
        in_specs=in_specs,
        out_specs=qo,
        out_shape=jax.ShapeDtypeStruct((T, 256), BF16),
        input_output_aliases=aliases,
        compiler_params=_cparams(("parallel", "parallel", "parallel")),
        name=f"diff_attn_{L}",
    )(*args)


def _na_cols_body(rpb_ref, onehot_ref, mask_ref, o_ref):
    oh = onehot_ref[...]
    o_ref[...] = mask_ref[...] + LOG2E * sum(jnp.dot(part, oh, preferred_element_type=F32)
                                             for part in _split3(rpb_ref[...]))


def _na_bias_table(rpb):
    nh, nr, ncol = rpb.shape
    qc = np.arange(GRID_W)[:, None]
    kc = np.arange(GRID_W)[None, :]
    sc = np.clip(qc - NA_WIN_C // 2, 0, GRID_W - NA_WIN_C)
    valid = ((kc >= sc) & (kc < sc + NA_WIN_C)).reshape(-1)
    dc = (kc - qc + (NA_WIN_C - 1)).reshape(-1)
    onehot = (np.arange(128)[:, None] == dc[None, :]) & valid[None, :]
    mask = np.where(valid, 0.0, NEG)[None, :]
    rows = 64
    rpb2 = jnp.pad(rpb.astype(F32).reshape(nh * nr, ncol), ((0, rows - nh * nr), (0, 128 - ncol)))
    cols = pl.pallas_call(
        _na_cols_body,
        out_shape=jax.ShapeDtypeStruct((rows, GRID_W * GRID_W), F32),
        name="na_bias_cols",
    )(rpb2, jnp.asarray(onehot, BF16), jnp.asarray(mask, F32))
    cols = cols[:nh * nr].reshape(nh, nr, GRID_W, GRID_W)
    per_type = [jnp.transpose(cols[:, NA_WIN_R - 1 - d:2 * NA_WIN_R - 1 - d], (0, 2, 1, 3)) for d in range(8)]
    return jnp.stack(per_type).reshape(8, 2, 2 * GRID_W, NA_WIN_R * GRID_W)


NA_ROWS_PER_ITER = 4


def _na_body(bias_ref, q_ref, k_ref, v_ref, o_ref, *, rows):
    lane = lax.broadcasted_iota(jnp.int32, (GRID_W, 128), 1)
    scale = NA_HDIM ** -0.5 * LOG2E

    def row_group(g, carry):
        rs = [g * NA_ROWS_PER_ITER + u for u in range(NA_ROWS_PER_ITER)]
        s0s = [jnp.clip(r - NA_WIN_R // 2, 0, rows - NA_WIN_R) for r in rs]
        wins = [pl.ds(pl.multiple_of(s0 * GRID_W, GRID_W), NA_WIN_R * GRID_W) for s0 in s0s]
        qrows = [pl.ds(pl.multiple_of(r * GRID_W, GRID_W), GRID_W) for r in rs]
        scores = []
        for r, s0, win, qrow in zip(rs, s0s, wins, qrows):
            q = q_ref[qrow, :]
            zero = jnp.zeros_like(q)
            qs = jnp.concatenate([jnp.where(lane < 64, q, zero), jnp.where(lane >= 64, q, zero)], axis=0)
            s = lax.dot_general(qs, k_ref[win, :], (((1,), (1,)), ((), ())), preferred_element_type=F32)
            scores.append(s * scale + bias_ref[r - s0])
        es = [jnp.exp2(s - jnp.max(s, axis=-1, keepdims=True)) for s in scores]
        outs = []
        for e, win in zip(es, wins):
            o2 = jnp.dot(e.astype(BF16), v_ref[win, :], preferred_element_type=F32)
            o2 = o2 * (1.0 / jnp.sum(e, axis=-1, keepdims=True))
            outs.append(jnp.where(lane < 64, o2[:GRID_W], o2[GRID_W:]).astype(BF16))
        for o, qrow in zip(outs, qrows):
            o_ref[qrow, :] = o
        return carry

    assert rows % NA_ROWS_PER_ITER == 0
    lax.fori_loop(0, rows // NA_ROWS_PER_ITER, row_group, 0)


def _nbr_attn(q, k, v, bias, o_prev, seg):
    row0, bsz, L = seg
    T = q.shape[0]
    blk0, rows = row0 // L, L // GRID_W
    assert rows >= NA_WIN_R
    blk = pl.BlockSpec((L, 128), lambda b, p: (blk0 + b, p))
    args = [bias, q, k, v]
    in_specs = [pl.BlockSpec((8, None, 128, 512), lambda b, p: (0, p, 0, 0)), blk, blk, blk]
    aliases = {}
    if o_prev is not None:
        args.append(o_prev)
        in_specs.append(pl.BlockSpec(memory_space=pl.ANY))
        aliases = {4: 0}

    def body(*refs):
        _na_body(*refs[:4], refs[-1], rows=rows)

    return pl.pallas_call(
        body,
        grid=(bsz, 2),
        in_specs=in_specs,
        out_specs=blk,
        out_shape=jax.ShapeDtypeStruct((T, 256), BF16),
        input_output_aliases=aliases,
        compiler_params=_cparams(("parallel", "parallel")),
        name=f"nbr_attn_{L}",
    )(*args)


CONV_HALO = 8


def _conv_body(w_ref, b_ref, prev_ref, cur_ref, next_ref, o_ref, buf_ref, *, segs, tc):
    i = pl.program_id(0)
    pos, per = _seg_pos(i, segs, tc)
    buf_ref[0:CONV_HALO, :] = prev_ref[...]
    buf_ref[CONV_HALO:CONV_HALO + tc, :] = cur_ref[...]
    buf_ref[CONV_HALO + tc:, :] = next_ref[...]

    @pl.when(pos == 0)
    def _():
        buf_ref[0:CONV_HALO, :] = jnp.zeros((CONV_HALO, SSM_XBC), F32)

    @pl.when(pos == per - 1)
    def _():
        buf_ref[CONV_HALO + tc:, :] = jnp.zeros((CONV_HALO, SSM_XBC), F32)

    rc = 64
    for r0 in range(0, tc, rc):
        acc = jnp.broadcast_to(b_ref[...], (rc, SSM_XBC))
        for j in range(SSM_CONV):
            off = CONV_HALO - SSM_CONV // 2 + j + r0
            acc = acc + buf_ref[off:off + rc, :] * w_ref[j:j + 1, :]
        o_ref[r0:r0 + rc, :] = (acc * jax.nn.sigmoid(acc)).astype(BF16)


def _conv_silu(xbc, w8, b, segs, tc):
    T = xbc.shape[0]
    hb = tc // CONV_HALO
    last = T // CONV_HALO - 1
    return pl.pallas_call(
        functools.partial(_conv_body, segs=segs, tc=tc),
        grid=(T // tc,),
        in_specs=[_const_spec((8, SSM_XBC)), _const_spec((1, SSM_XBC)),
                  pl.BlockSpec((CONV_HALO, SSM_XBC), lambda i: (jnp.maximum(i * hb - 1, 0), 0)),
                  pl.BlockSpec((tc, SSM_XBC), lambda i: (i, 0)),
                  pl.BlockSpec((CONV_HALO, SSM_XBC), lambda i: (jnp.minimum((i + 1) * hb, last), 0))],
        out_specs=pl.BlockSpec((tc, SSM_XBC), lambda i: (i, 0)),
        out_shape=jax.ShapeDtypeStruct((T, SSM_XBC), BF16),
        scratch_shapes=[pltpu.VMEM((tc + 2 * CONV_HALO, SSM_XBC), F32)],
        compiler_params=_cparams(("parallel",)),
        name="ssd_conv",
    )(w8, b, xbc, xbc, xbc)


def _split3(x):
    hi = x.astype(BF16)
    r1 = x - hi.astype(F32)
    mid = r1.astype(BF16)
    lo = (r1 - mid.astype(F32)).astype(BF16)
    return hi, mid, lo


def _ssd_body(dtb_ref, alog_ref, dskip_ref, ng_ref, act_ref, z_ref, dt_ref, o_ref, y_ref, h_ref, *, L):
    C = SSM_CHUNK
    nc = L // C
    rix = lax.broadcasted_iota(jnp.int32, (C, C), 0)
    cix = lax.broadcasted_iota(jnp.int32, (C, C), 1)
    left = cix < 64
    dtb = dtb_ref[...]
    a = -jnp.exp(alog_ref[...])

    def pair(u, c0):
        return jnp.where(left[:u.shape[0]], u[:, c0:c0 + 1], u[:, c0 + 1:c0 + 2])

    def prep(c, direction):
        rows = pl.ds(pl.multiple_of(c * C, C), C)
        keep = (cix <= rix) if direction == 0 else (cix >= rix)
        dt = jax.nn.softplus(dt_ref[rows, :] + dtb)
        tri = jnp.where(keep, 1.0, 0.0).astype(BF16)
        acs = sum(jnp.dot(tri, part, preferred_element_type=F32) for part in _split3(dt * a))
        return rows, keep, dt, acs, acs.T

    def job(shared, direction, g):
        rows, keep, dt, acs, acs_t = shared
        c0 = 4 * direction + 2 * g
        end = C - 1 if direction == 0 else 0
        xg = act_ref[rows, 128 * g:128 * (g + 1)].astype(F32)
        bg = act_ref[rows, 256 + 128 * g:256 + 128 * (g + 1)]
        cg = act_ref[rows, 512 + 128 * g:512 + 128 * (g + 1)]
        cb = lax.dot_general(cg, bg, (((1,), (1,)), ((), ())), preferred_element_type=F32)
        h_prev = h_ref[direction, g]
        y_off = jnp.dot(cg, h_prev.astype(BF16), preferred_element_type=F32)
        yield
        xdt = xg * pair(dt, c0)
        xdt16 = xdt.astype(BF16)
        bt = bg.astype(F32).T.astype(BF16)
        acs_end = jnp.where(left[:1], acs_t[c0:c0 + 1, end:end + 1], acs_t[c0 + 1:c0 + 2, end:end + 1])
        acs_p = pair(acs, c0)
        w = (xdt * jnp.exp(acs_end - acs_p)).astype(BF16)
        st = jnp.dot(bt, w, preferred_element_type=F32)
        yield
        yd = []
        for hh in range(2):
            seg = acs[:, c0 + hh:c0 + hh + 1] - acs_t[c0 + hh:c0 + hh + 1, :]
            m = cb * jnp.exp(jnp.where(keep, seg, NEG))
            yd.append(jnp.dot(m.astype(BF16), xdt16, preferred_element_type=F32))
        yield
        h_ref[direction, g] = h_prev * jnp.exp(acs_end) + st
        y_ref[direction, rows, 128 * g:128 * (g + 1)] = jnp.where(left, yd[0], yd[1]) + y_off * jnp.exp(acs_p)
        yield

    def finish(c, carry):
        rows = pl.ds(pl.multiple_of(c * C, C), C)
        xs = act_ref[rows, 0:256].astype(F32)
        y = y_ref[0, rows, :] + y_ref[1, rows, :] + xs * dskip_ref[...]
        zf = z_ref[rows, :].astype(F32)
        y = y * (zf * jax.nn.sigmoid(zf))
        y = jnp.concatenate([_rms_rows(y[:, :128]), _rms_rows(y[:, 128:])], axis=1)
        o_ref[rows, :] = (y * ng_ref[...]).astype(BF16)
        return carry

    def step(t, carry):
        shared = [prep(t, 0), prep(nc - 1 - t, 1)]
        jobs = [job(shared[d], d, g) for d in range(2) for g in range(2)]
        for _ in range(4):
            for j in jobs:
                next(j)
        return carry

    h_ref[...] = jnp.zeros_like(h_ref)
    lax.fori_loop(0, nc, step, 0)
    lax.fori_loop(0, nc, finish, 0)


def _ssd_scan(act, z, dt, dtb, alog, dskip, ng, o_prev, seg):
    row0, bsz, L = seg
    T = act.shape[0]
    blk0 = row0 // L
    blk = lambda n: pl.BlockSpec((L, n), lambda b: (blk0 + b, 0))
    args = [dtb, alog, dskip, ng, act, z, dt]
    in_specs = [_const_spec((1, 128)), _const_spec((1, 128)), _const_spec((1, 256)), _const_spec((1, 256)),
                blk(SSM_XBC), blk(256), blk(128)]
    aliases = {}
    if o_prev is not None:
        args.append(o_prev)
        in_specs.append(pl.BlockSpec(memory_space=pl.ANY))
        aliases = {7: 0}

    def body(*refs):
        n = len(args)
        _ssd_body(*refs[:7], *refs[n:], L=L)

    return pl.pallas_call(
        body,
        grid=(bsz,),
        in_specs=in_specs,
        out_specs=blk(256),
        out_shape=jax.ShapeDtypeStruct((T, 256), BF16),
        scratch_shapes=[pltpu.VMEM((2, L, 256), F32), pltpu.VMEM((2, 2, 128, 128), F32)],
        input_output_aliases=aliases,
        compiler_params=_cparams(("parallel",)),
        name=f"ssd_scan_{L}",
    )(*args)


FF_CHUNK = 256


def _out_ffn_body(x_ref, ofn_ref, oda_ref, ona_ref, ossm_ref, wo_ref, g_ref, wg_ref, wu_ref, wd_ref, fg_ref,
                  o_ref, *, final):
    mix = jnp.concatenate([ofn_ref[...], oda_ref[...], ona_ref[...], ossm_ref[...]], axis=1)
    x = x_ref[...] + jnp.dot(mix, wo_ref[...], preferred_element_type=F32)
    h = (_rms_rows(x) * g_ref[...]).astype(BF16)
    gate = jnp.dot(h, wg_ref[...], preferred_element_type=F32)
    up = jnp.dot(h, wu_ref[...], preferred_element_type=F32)
    act = (gate * jax.nn.sigmoid(gate) * up).astype(BF16)
    x = x + jnp.dot(act, wd_ref[...], preferred_element_type=F32)
    if final:
        x = _rms_rows(x) * fg_ref[...]
    o_ref[...] = x


def _out_ffn(x, mixes, wo, g, wg, wu, wd, fg, tm, final):
    T = x.shape[0]
    row = lambda n: pl.BlockSpec((tm, n), lambda i: (i, 0))
    return pl.pallas_call(
        functools.partial(_out_ffn_body, final=final),
        grid=(T // tm,),
        in_specs=[row(D_MODEL)] + [row(256)] * 4 +
                 [_const_spec((D_MODEL, D_MODEL)), _const_spec((1, D_MODEL)), _const_spec((D_MODEL, D_FF)),
                  _const_spec((D_MODEL, D_FF)), _const_spec((D_FF, D_MODEL)), _const_spec((1, D_MODEL))],
        out_specs=row(D_MODEL),
        out_shape=jax.ShapeDtypeStruct((T, D_MODEL), F32),
        input_output_aliases={0: 0},
        compiler_params=_cparams(("parallel",)),
        name="out_ffn",
    )(x, *mixes, wo, g, wg, wu, wd, fg)


def _channel_dft():
    c = np.arange(GROUP_W)
    same = (c[:, None] // FN_CH) == (c[None, :] // FN_CH)
    ang = 2.0 * np.pi * ((c[:, None] % FN_CH) * (c[None, :] % FN_CH) % FN_CH) / FN_CH
    cs = np.concatenate([np.where(same, np.cos(ang), 0.0), np.where(same, np.sin(ang), 0.0)], axis=1)
    return jnp.asarray(cs / math.sqrt(FN_CH), BF16)


def _seq_dft(L):
    split = 64
    l = jnp.arange(L, dtype=jnp.int32)[None, :]
    ka = jnp.arange(L // split, dtype=jnp.int32)[:, None] * split
    kb = jnp.arange(split, dtype=jnp.int32)[:, None]
    ang = lambda kk: (2.0 * math.pi / L) * lax.rem(kk * l, L).astype(F32)
    ca, sa, cb, sb = jnp.cos(ang(ka)), jnp.sin(ang(ka)), jnp.cos(ang(kb)), jnp.sin(ang(kb))
    cc = ca[:, None, :] * cb[None, :, :] - sa[:, None, :] * sb[None, :, :]
    ss = sa[:, None, :] * cb[None, :, :] + ca[:, None, :] * sb[None, :, :]
    tab = jnp.concatenate([cc.reshape(L, L), -ss.reshape(L, L)], axis=1) * (L ** -0.5)
    return tab.astype(BF16)


def _rope_tables(max_len):
    half = DA_QDIM // 2
    pos = jnp.arange(max_len, dtype=F32)
    inv = jnp.power(ROPE_THETA, -jnp.arange(half, dtype=F32) * (2.0 / DA_QDIM))
    ang = pos[:, None] * inv[None, :]
    cos, sin = jnp.cos(ang), jnp.sin(ang)
    zero = jnp.zeros_like(sin)
    reps = 128 // DA_QDIM
    return (jnp.tile(jnp.concatenate([cos, cos], axis=1), (1, reps)),
            jnp.tile(jnp.concatenate([-sin, zero], axis=1), (1, reps)),
            jnp.tile(jnp.concatenate([zero, sin], axis=1), (1, reps)))


def _trunk(x, segs, p, tm_proj=512, tm_ffn=256, tm_dft=512, tq=256, tc=512):
    max_len = max(s[2] for s in segs)
    rope_tabs = _rope_tables(max_len)
    cdft = _channel_dft()
    dft_tabs = {L: _seq_dft(L) for L in sorted({s[2] for s in segs})}
    c = np.arange(128)
    gmat = jnp.asarray(((c[:, None] // 64) == (c[None, :] // 64)) / 64.0, BF16)
    row2 = lambda v: v.reshape(1, -1).astype(F32)
    pad128 = lambda v: jnp.pad(v.reshape(1, -1).astype(F32), ((0, 0), (0, 128 - v.size)))

    def over_segments(fn):
        out = None
        for seg in segs:
            out = fn(out, seg)
        return out

    for i in range(DEPTH):
        lambda_init = 0.8 - 0.6 * math.exp(-0.3 * i)
        w_in = jnp.pad(p['w_in'][i], ((0, 0), (0, D_IN_PAD - D_IN))).astype(BF16)
        (xcs, daq, dak, dav, naq, nak, nav, z, xbc, dt) = _in_proj(
            x, row2(p['attn_norm_g'][i]), w_in, cdft, rope_tabs, segs, tm_proj)

        wf = p['w_fourier'][i].astype(BF16)
        o_fn = over_segments(lambda o, seg: _fourier(xcs, dft_tabs[seg[2]], wf, o, seg, min(tm_dft, seg[2])))

        g128 = jnp.tile(row2(p['diff_subln_g'][i]), (1, 2))
        lam_vecs = p['diff_lambda'][i].astype(F32)
        o_da = over_segments(lambda o, seg: _diff_attn(daq, dak, dav, lam_vecs, g128, gmat, o, seg,
                                                       min(tq, seg[2]), lambda_init))

        bias = _na_bias_table(p['na_rpb'][i])
        o_na = over_segments(lambda o, seg: _nbr_attn(naq, nak, nav, bias, o, seg))

        w8 = jnp.pad(p['ssm_conv_w'][i].astype(F32), ((0, 8 - SSM_CONV), (0, 0)))
        act = _conv_silu(xbc, w8, row2(p['ssm_conv_b'][i]), segs, tc)
        dtb, alog = pad128(p['ssm_dt_bias'][i]), pad128(p['ssm_A_log'][i])
        dskip = jnp.repeat(row2(p['ssm_D'][i]), GROUP_W // SSM_HEADS, axis=1)
        ng = row2(p['ssm_norm_g'][i])
        o_ssm = over_segments(lambda o, seg: _ssd_scan(act, z, dt, dtb, alog, dskip, ng, o, seg))

        x = _out_ffn(x, (o_fn, o_da, o_na, o_ssm), p['w_out'][i].astype(BF16), row2(p['ffn_norm_g'][i]),
                     p['w_gate'][i].astype(BF16), p['w_up'][i].astype(BF16), p['w_down'][i].astype(BF16),
                     row2(p['final_norm_g']), tm_ffn, final=(i == DEPTH - 1))
    return x


def kernel(x_prompt, x_sample, attn_norm_g, w_in, w_fourier, diff_lambda, diff_subln_g, na_rpb, ssm_conv_w,
           ssm_conv_b, ssm_dt_bias, ssm_A_log, ssm_D, ssm_norm_g, w_out, ffn_norm_g, w_gate, w_up, w_down,
           final_norm_g):
    params = dict(attn_norm_g=attn_norm_g, w_in=w_in, w_fourier=w_fourier, diff_lambda=diff_lambda,
                  diff_subln_g=diff_subln_g, na_rpb=na_rpb, ssm_conv_w=ssm_conv_w, ssm_conv_b=ssm_conv_b,
                  ssm_dt_bias=ssm_dt_bias, ssm_A_log=ssm_A_log, ssm_D=ssm_D, ssm_norm_g=ssm_norm_g,
                  w_out=w_out, ffn_norm_g=ffn_norm_g, w_gate=w_gate, w_up=w_up, w_down=w_down,
                  final_norm_g=final_norm_g)
    bp, lp, _ = x_prompt.shape
    bs, ls, _ = x_sample.shape
    n_p = bp * lp
    segs = ((0, bp, lp), (n_p, bs, ls))
    x = jnp.concatenate([x_prompt.reshape(n_p, D_MODEL), x_sample.reshape(bs * ls, D_MODEL)], axis=0)
    y = _trunk(x, segs, params)
    return y[:n_p].reshape(x_prompt.shape), y[n_p:].reshape(x_sample.shape)
```

```python
import functools
import math

import numpy as np
import jax
import jax.numpy as jnp
from jax import lax
from jax.experimental import pallas as pl
from jax.experimental.pallas import tpu as pltpu

F32 = jnp.float32
BF16 = jnp.bfloat16

D_MODEL = 1024
DEPTH = 4
GRID_W = 64
GROUP_W = D_MODEL // 4
FN_CH = 64
DA_QDIM = 32
ROPE_THETA = 10000.0
NA_HDIM = 64
NA_WIN_R = 8
NA_WIN_C = 16
SSM_HEADS = 4
SSM_CONV = 5
SSM_CHUNK = 128
SSM_XBC = 768
D_IN = 2824
D_IN_PAD = 2944
D_FF = 2816
EPS = 1e-6
NEG = -1e30
LOG2E = math.log2(math.e)

V7X_VMEM_LIMIT = 56 * 1024 * 1024


def _cparams(sem):
    return pltpu.CompilerParams(dimension_semantics=sem, vmem_limit_bytes=V7X_VMEM_LIMIT)


def _const_spec(shape):
    nd = len(shape)
    return pl.BlockSpec(shape, lambda *_: (0,) * nd, pipeline_mode=pl.Buffered(1))


def _rms_rows(x):
    return x * lax.rsqrt(jnp.mean(x * x, axis=-1, keepdims=True) + EPS)


def _seg_pos(i, segs, tile):
    pos, per = None, None
    for row0, bsz, seq in reversed(segs):
        t0, n = row0 // tile, seq // tile
        p = lax.rem(i - t0, n)
        if pos is None:
            pos, per = p, jnp.full((), n, jnp.int32)
        else:
            end = (row0 + bsz * seq) // tile
            pos = jnp.where(i < end, p, pos)
            per = jnp.where(i < end, n, per)
    return pos, per


def _group_sq_norm(x16, nmat_ref):
    xf = x16.astype(F32)
    sq = xf * xf
    hi = sq.astype(BF16)
    lo = (sq - hi.astype(F32)).astype(BF16)
    n = nmat_ref[...]
    return jnp.dot(hi, n, preferred_element_type=F32) + jnp.dot(lo, n, preferred_element_type=F32)


def _in_proj_body(x_ref, g_ref, w_ref, cdft_ref, nmat_ref, cos_ref, sa_ref, sb_ref,
                  xcs_ref, daq_ref, dak_ref, dav_ref, naq_ref, nak_ref, nav_ref, z_ref, xbc_ref, dt_ref, nrm_ref):
    h = (_rms_rows(x_ref[...]) * g_ref[...]).astype(BF16)

    def proj(a, b):
        return jnp.dot(h, w_ref[:, a:b], preferred_element_type=F32)

    fn = proj(0, 256)
    xcs_ref[...] = jnp.dot(fn.astype(BF16), cdft_ref[...], preferred_element_type=F32).astype(BF16)

    cos, sa, sb = cos_ref[...], sa_ref[...], sb_ref[...]

    def rope(u):
        return u * cos + pltpu.roll(u, 112, 1) * sa + pltpu.roll(u, 16, 1) * sb

    scale = DA_QDIM ** -0.5 * LOG2E
    nrm_ref[...] = jnp.zeros_like(nrm_ref)
    for half in range(2):
        c0 = 256 + 128 * half
        cols = slice(128 * half, 128 * (half + 1))
        q16 = (rope(proj(c0, c0 + 128)) * scale).astype(BF16)
        k16 = rope(proj(c0 + 256, c0 + 384)).astype(BF16)
        daq_ref[:, cols] = q16
        dak_ref[:, cols] = k16
        nrm_ref[0:1, cols] = jnp.max(_group_sq_norm(q16, nmat_ref), axis=0, keepdims=True)
        nrm_ref[1:2, cols] = jnp.max(_group_sq_norm(k16, nmat_ref), axis=0, keepdims=True)
    dav_ref[...] = proj(768, 1024).astype(BF16)
    naq_ref[...] = proj(1024, 1280).astype(BF16)
    nak_ref[...] = proj(1280, 1536).astype(BF16)
    nav_ref[...] = proj(1536, 1792).astype(BF16)
    z_ref[...] = proj(1792, 2048).astype(BF16)
    xbc_ref[...] = proj(2048, 2816)
    dt_ref[...] = proj(2816, 2944)


def _in_proj(x, g, w, cdft, nmat, rope_tabs, segs, tm):
    T = x.shape[0]
    cos, sa, sb = rope_tabs

    def tab_map(i):
        pos, _ = _seg_pos(i, segs, tm)
        return (pos, 0)

    row = lambda n: pl.BlockSpec((tm, n), lambda i: (i, 0))
    tab = pl.BlockSpec((tm, 128), tab_map)
    outs = [(512, BF16)] + [(256, BF16)] * 7 + [(SSM_XBC, F32), (128, F32)]
    return pl.pallas_call(
        _in_proj_body,
        grid=(T // tm,),
        in_specs=[row(D_MODEL), _const_spec((1, D_MODEL)), _const_spec((D_MODEL, D_IN_PAD)),
                  _const_spec((256, 512)), _const_spec((128, 128)), tab, tab, tab],
        out_specs=[row(n) for n, _ in outs] + [pl.BlockSpec((8, 256), lambda i: (i, 0))],
        out_shape=[jax.ShapeDtypeStruct((T, n), dt) for n, dt in outs] +
                  [jax.ShapeDtypeStruct((T // tm * 8, 256), F32)],
        compiler_params=_cparams(("parallel",)),
        name="in_proj",
    )(x, g, w, cdft, nmat, cos, sa, sb)


def _fourier_body(tab_ref, xcs_ref, w_ref, o_ref, *, L):
    acc = jnp.dot(tab_ref[:, :L], xcs_ref[:, :256], preferred_element_type=F32)
    acc += jnp.dot(tab_ref[:, L:], xcs_ref[:, 256:], preferred_element_type=F32)
    o_ref[...] = jnp.dot(acc.astype(BF16), w_ref[...], preferred_element_type=F32).astype(BF16)


def _fourier(xcs, tab, w, bsz, L, tm):
    mt = L // tm
    return pl.pallas_call(
        functools.partial(_fourier_body, L=L),
        grid=(mt, bsz),
        in_specs=[pl.BlockSpec((tm, 2 * L), lambda m, b: (m, 0)),
                  pl.BlockSpec((L, 512), lambda m, b: (b, 0)),
                  _const_spec((256, 256))],
        out_specs=pl.BlockSpec((tm, 256), lambda m, b: (b * mt + m, 0)),
        out_shape=jax.ShapeDtypeStruct((bsz * L, 256), BF16),
        compiler_params=_cparams(("parallel", "parallel")),
        name=f"fourier_{L}",
    )(tab, xcs, w)


def _group_mean_sq(o, gmat_ref):
    sq = o * o
    hi = sq.astype(BF16)
    lo = (sq - hi.astype(F32)).astype(BF16)
    g = gmat_ref[...]
    return jnp.dot(hi, g, preferred_element_type=F32) + jnp.dot(lo, g, preferred_element_type=F32)


DA_NOSHIFT_LOG2_LIMIT = 48.0


def _da_noshift_flags(nrm, bsz):
    m = jnp.max(nrm.reshape(bsz, -1, 8, 256), axis=1)
    bound_sq = jnp.max((m[:, 0] * m[:, 1]).reshape(bsz, 2, 128), axis=-1) * 1.01
    return (bound_sq <= DA_NOSHIFT_LOG2_LIMIT ** 2).astype(jnp.int32)


def _da_body(noshift_ref, lam_ref, g_ref, gmat_ref, q_ref, k_ref, v_ref, o_ref, *, lambda_init):
    lv = lam_ref[...]
    l1 = jnp.sum(lv[0:1] * lv[1:2], axis=-1, keepdims=True)
    l2 = jnp.sum(lv[2:3] * lv[3:4], axis=-1, keepdims=True)
    lam = jnp.exp(l1) - jnp.exp(l2) + lambda_init

    def attend(shift):
        q, k, v = q_ref[...], k_ref[...], v_ref[...]
        lane = lax.broadcasted_iota(jnp.int32, q.shape, 1)

        def scores(j):
            out = []
            for t in range(2):
                lo = 64 * j + 32 * t
                qm = jnp.where((lane >= lo) & (lane < lo + 32), q, jnp.zeros_like(q))
                out.append(lax.dot_general(qm, k, (((1,), (1,)), ((), ())), preferred_element_type=F32))
            return out

        def head(ss):
            if shift:
                ss = [s - jnp.max(s, axis=-1, keepdims=True) for s in ss]
            es = [jnp.exp2(s) for s in ss]
            sums = [jnp.sum(e, axis=-1, keepdims=True) for e in es]
            a = (es[0] - (lam * sums[0] / sums[1]) * es[1]).astype(BF16)
            return jnp.dot(a, v, preferred_element_type=F32) * (1.0 / sums[0])

        s0, s1 = scores(0), scores(1)
        o = jnp.where(lane < 64, head(s0), head(s1))
        o = o * lax.rsqrt(_group_mean_sq(o, gmat_ref) + EPS) * g_ref[...]
        o_ref[...] = (o * (1.0 - lambda_init)).astype(BF16)

    noshift = noshift_ref[pl.program_id(0), pl.program_id(1)] == 1
    pl.when(noshift)(lambda: attend(False))
    pl.when(jnp.logical_not(noshift))(lambda: attend(True))


def _diff_attn(q, k, v, noshift, lam_vecs, g128, gmat, bsz, L, tq, lambda_init):
    nq = L // tq
    kv = pl.BlockSpec((L, 128), lambda b, p, i, flags: (b, p))
    qo = pl.BlockSpec((tq, 128), lambda b, p, i, flags: (b * nq + i, p))
    return pl.pallas_call(
        functools.partial(_da_body, lambda_init=lambda_init),
        grid_spec=pltpu.PrefetchScalarGridSpec(
            num_scalar_prefetch=1,
            grid=(bsz, 2, nq),
            in_specs=[_const_spec((4, DA_QDIM)), _const_spec((1, 128)), _const_spec((128, 128)), qo, kv, kv],
            out_specs=qo),
        out_shape=jax.ShapeDtypeStruct((bsz * L, 256), BF16),
        compiler_params=_cparams(("parallel", "parallel", "parallel")),
        name=f"diff_attn_{L}",
    )(noshift, lam_vecs, g128, gmat, q, k, v)


def _na_cols_body(rpb_ref, onehot_ref, mask_ref, o_ref):
    oh = onehot_ref[...]
    o_ref[...] = mask_ref[...] + LOG2E * sum(jnp.dot(part, oh, preferred_element_type=F32)
                                             for part in _split3(rpb_ref[...]))


def _na_bias_table(rpb):
    nh, nr, ncol = rpb.shape
    qc = np.arange(GRID_W)[:, None]
    kc = np.arange(GRID_W)[None, :]
    sc = np.clip(qc - NA_WIN_C // 2, 0, GRID_W - NA_WIN_C)
    valid = ((kc >= sc) & (kc < sc + NA_WIN_C)).reshape(-1)
    dc = (kc - qc + (NA_WIN_C - 1)).reshape(-1)
    onehot = (np.arange(128)[:, None] == dc[None, :]) & valid[None, :]
    mask = np.where(valid, 0.0, NEG)[None, :]
    rows = 64
    rpb2 = jnp.pad(rpb.astype(F32).reshape(nh * nr, ncol), ((0, rows - nh * nr), (0, 128 - ncol)))
    cols = pl.pallas_call(
        _na_cols_body,
        out_shape=jax.ShapeDtypeStruct((rows, GRID_W * GRID_W), F32),
        name="na_bias_cols",
    )(rpb2, jnp.asarray(onehot, BF16), jnp.asarray(mask, F32))
    cols = cols[:nh * nr].reshape(nh, nr, GRID_W, GRID_W)
    per_type = [jnp.transpose(cols[:, NA_WIN_R - 1 - d:2 * NA_WIN_R - 1 - d], (0, 2, 1, 3)) for d in range(8)]
    return jnp.stack(per_type).reshape(8, 2, 2 * GRID_W, NA_WIN_R * GRID_W)


NA_ROWS_PER_ITER = 4


def _na_body(bias_ref, q_ref, k_ref, v_ref, o_ref, *, rows):
    lane = lax.broadcasted_iota(jnp.int32, (GRID_W, 128), 1)
    scale = NA_HDIM ** -0.5 * LOG2E

    def row_group(g, carry):
        rs = [g * NA_ROWS_PER_ITER + u for u in range(NA_ROWS_PER_ITER)]
        s0s = [jnp.clip(r - NA_WIN_R // 2, 0, rows - NA_WIN_R) for r in rs]
        wins = [pl.ds(pl.multiple_of(s0 * GRID_W, GRID_W), NA_WIN_R * GRID_W) for s0 in s0s]
        qrows = [pl.ds(pl.multiple_of(r * GRID_W, GRID_W), GRID_W) for r in rs]
        scores = []
        for r, s0, win, qrow in zip(rs, s0s, wins, qrows):
            q = q_ref[qrow, :]
            zero = jnp.zeros_like(q)
            qs = jnp.concatenate([jnp.where(lane < 64, q, zero), jnp.where(lane >= 64, q, zero)], axis=0)
            s = lax.dot_general(qs, k_ref[win, :], (((1,), (1,)), ((), ())), preferred_element_type=F32)
            scores.append(s * scale + bias_ref[r - s0])
        es = [jnp.exp2(s - jnp.max(s, axis=-1, keepdims=True)) for s in scores]
        outs = []
        for e, win in zip(es, wins):
            o2 = jnp.dot(e.astype(BF16), v_ref[win, :], preferred_element_type=F32)
            o2 = o2 * (1.0 / jnp.sum(e, axis=-1, keepdims=True))
            outs.append(jnp.where(lane < 64, o2[:GRID_W], o2[GRID_W:]).astype(BF16))
        for o, qrow in zip(outs, qrows):
            o_ref[qrow, :] = o
        return carry

    assert rows % NA_ROWS_PER_ITER == 0
    lax.fori_loop(0, rows // NA_ROWS_PER_ITER, row_group, 0)


def _nbr_attn(q, k, v, bias, bsz, L):
    rows = L // GRID_W
    assert rows >= NA_WIN_R
    blk = pl.BlockSpec((L, 128), lambda b, p: (b, p))
    return pl.pallas_call(
        functools.partial(_na_body, rows=rows),
        grid=(bsz, 2),
        in_specs=[pl.BlockSpec((8, None, 128, 512), lambda b, p: (0, p, 0, 0)), blk, blk, blk],
        out_specs=blk,
        out_shape=jax.ShapeDtypeStruct((bsz * L, 256), BF16),
        compiler_params=_cparams(("parallel", "parallel")),
        name=f"nbr_attn_{L}",
    )(bias, q, k, v)


CONV_HALO = 8


def _conv_body(w_ref, b_ref, prev_ref, cur_ref, next_ref, o_ref, buf_ref, *, segs, tc):
    i = pl.program_id(0)
    pos, per = _seg_pos(i, segs, tc)
    buf_ref[0:CONV_HALO, :] = prev_ref[...]
    buf_ref[CONV_HALO:CONV_HALO + tc, :] = cur_ref[...]
    buf_ref[CONV_HALO + tc:, :] = next_ref[...]

    @pl.when(pos == 0)
    def _():
        buf_ref[0:CONV_HALO, :] = jnp.zeros((CONV_HALO, SSM_XBC), F32)

    @pl.when(pos == per - 1)
    def _():
        buf_ref[CONV_HALO + tc:, :] = jnp.zeros((CONV_HALO, SSM_XBC), F32)

    rc = 64
    for r0 in range(0, tc, rc):
        acc = jnp.broadcast_to(b_ref[...], (rc, SSM_XBC))
        for j in range(SSM_CONV):
            off = CONV_HALO - SSM_CONV // 2 + j + r0
            acc = acc + buf_ref[off:off + rc, :] * w_ref[j:j + 1, :]
        o_ref[r0:r0 + rc, :] = (acc * jax.nn.sigmoid(acc)).astype(BF16)


def _conv_silu(xbc, w8, b, segs, tc):
    T = xbc.shape[0]
    hb = tc // CONV_HALO
    last = T // CONV_HALO - 1
    return pl.pallas_call(
        functools.partial(_conv_body, segs=segs, tc=tc),
        grid=(T // tc,),
        in_specs=[_const_spec((8, SSM_XBC)), _const_spec((1, SSM_XBC)),
                  pl.BlockSpec((CONV_HALO, SSM_XBC), lambda i: (jnp.maximum(i * hb - 1, 0), 0)),
                  pl.BlockSpec((tc, SSM_XBC), lambda i: (i, 0)),
                  pl.BlockSpec((CONV_HALO, SSM_XBC), lambda i: (jnp.minimum((i + 1) * hb, last), 0))],
        out_specs=pl.BlockSpec((tc, SSM_XBC), lambda i: (i, 0)),
        out_shape=jax.ShapeDtypeStruct((T, SSM_XBC), BF16),
        scratch_shapes=[pltpu.VMEM((tc + 2 * CONV_HALO, SSM_XBC), F32)],
        compiler_params=_cparams(("parallel",)),
        name="ssd_conv",
    )(w8, b, xbc, xbc, xbc)


def _split3(x):
    hi = x.astype(BF16)
    r1 = x - hi.astype(F32)
    mid = r1.astype(BF16)
    lo = (r1 - mid.astype(F32)).astype(BF16)
    return hi, mid, lo


def _ssd_body(dtb_ref, alog_ref, dskip_ref, ng_ref, act_ref, z_ref, dt_ref, o_ref, y_ref, h_ref, *, L):
    C = SSM_CHUNK
    nc = L // C
    rix = lax.broadcasted_iota(jnp.int32, (C, C), 0)
    cix = lax.broadcasted_iota(jnp.int32, (C, C), 1)
    left = cix < 64
    dtb = dtb_ref[...]
    a = -jnp.exp(alog_ref[...])

    def pair(u, c0):
        return jnp.where(left[:u.shape[0]], u[:, c0:c0 + 1], u[:, c0 + 1:c0 + 2])

    def prep(c, direction):
        rows = pl.ds(pl.multiple_of(c * C, C), C)
        keep = (cix <= rix) if direction == 0 else (cix >= rix)
        dt = jax.nn.softplus(dt_ref[rows, :] + dtb)
        tri = jnp.where(keep, 1.0, 0.0).astype(BF16)
        acs = sum(jnp.dot(tri, part, preferred_element_type=F32) for part in _split3(dt * a))
        return rows, keep, dt, acs, acs.T

    def job(shared, direction, g):
        rows, keep, dt, acs, acs_t = shared
        c0 = 4 * direction + 2 * g
        end = C - 1 if direction == 0 else 0
        xg = act_ref[rows, 128 * g:128 * (g + 1)].astype(F32)
        bg = act_ref[rows, 256 + 128 * g:256 + 128 * (g + 1)]
        cg = act_ref[rows, 512 + 128 * g:512 + 128 * (g + 1)]
        cb = lax.dot_general(cg, bg, (((1,), (1,)), ((), ())), preferred_element_type=F32)
        h_prev = h_ref[direction, g]
        y_off = jnp.dot(cg, h_prev.astype(BF16), preferred_element_type=F32)
        yield
        xdt = xg * pair(dt, c0)
        xdt16 = xdt.astype(BF16)
        bt = bg.astype(F32).T.astype(BF16)
        acs_end = jnp.where(left[:1], acs_t[c0:c0 + 1, end:end + 1], acs_t[c0 + 1:c0 + 2, end:end + 1])
        acs_p = pair(acs, c0)
        w = (xdt * jnp.exp(acs_end - acs_p)).astype(BF16)
        st = jnp.dot(bt, w, preferred_element_type=F32)
        yield
        yd = []
        for hh in range(2):
            seg = acs[:, c0 + hh:c0 + hh + 1] - acs_t[c0 + hh:c0 + hh + 1, :]
            m = cb * jnp.exp(jnp.where(keep, seg, NEG))
            yd.append(jnp.dot(m.astype(BF16), xdt16, preferred_element_type=F32))
        yield
        h_ref[direction, g] = h_prev * jnp.exp(acs_end) + st
        y_ref[direction, rows, 128 * g:128 * (g + 1)] = jnp.where(left, yd[0], yd[1]) + y_off * jnp.exp(acs_p)
        yield

    def finish(c, carry):
        rows = pl.ds(pl.multiple_of(c * C, C), C)
        xs = act_ref[rows, 0:256].astype(F32)
        y = y_ref[0, rows, :] + y_ref[1, rows, :] + xs * dskip_ref[...]
        zf = z_ref[rows, :].astype(F32)
        y = y * (zf * jax.nn.sigmoid(zf))
        y = jnp.concatenate([_rms_rows(y[:, :128]), _rms_rows(y[:, 128:])], axis=1)
        o_ref[rows, :] = (y * ng_ref[...]).astype(BF16)
        return carry

    def step(t, carry):
        shared = [prep(t, 0), prep(nc - 1 - t, 1)]
        jobs = [job(shared[d], d, g) for d in range(2) for g in range(2)]
        for _ in range(4):
            for j in jobs:
                next(j)
        return carry

    h_ref[...] = jnp.zeros_like(h_ref)
    lax.fori_loop(0, nc, step, 0)
    lax.fori_loop(0, nc, finish, 0)


def _ssd_scan(act, z, dt, dtb, alog, dskip, ng, bsz, L):
    blk = lambda n: pl.BlockSpec((L, n), lambda b: (b, 0))
    return pl.pallas_call(
        functools.partial(_ssd_body, L=L),
        grid=(bsz,),
        in_specs=[_const_spec((1, 128)), _const_spec((1, 128)), _const_spec((1, 256)), _const_spec((1, 256)),
                  blk(SSM_XBC), blk(256), blk(128)],
        out_specs=blk(256),
        out_shape=jax.ShapeDtypeStruct((bsz * L, 256), BF16),
        scratch_shapes=[pltpu.VMEM((2, L, 256), F32), pltpu.VMEM((2, 2, 128, 128), F32)],
        compiler_params=_cparams(("parallel",)),
        name=f"ssd_scan_{L}",
    )(dtb, alog, dskip, ng, act, z, dt)


FF_CHUNK = 256


def _out_ffn_body(x_ref, ofn_ref, oda_ref, ona_ref, ossm_ref, wo_ref, g_ref, wg_ref, wu_ref, wd_ref, fg_ref,
                  o_ref, *, final):
    mix = jnp.concatenate([ofn_ref[...], oda_ref[...], ona_ref[...], ossm_ref[...]], axis=1)
    x = x_ref[...] + jnp.dot(mix, wo_ref[...], preferred_element_type=F32)
    h = (_rms_rows(x) * g_ref[...]).astype(BF16)
    gate = jnp.dot(h, wg_ref[...], preferred_element_type=F32)
    up = jnp.dot(h, wu_ref[...], preferred_element_type=F32)
    act = (gate * jax.nn.sigmoid(gate) * up).astype(BF16)
    x = x + jnp.dot(act, wd_ref[...], preferred_element_type=F32)
    if final:
        x = _rms_rows(x) * fg_ref[...]
    o_ref[...] = x


def _out_ffn(x, mixes, wo, g, wg, wu, wd, fg, tm, final, in_place):
    T = x.shape[0]
    row = lambda n: pl.BlockSpec((tm, n), lambda i: (i, 0))
    return pl.pallas_call(
        functools.partial(_out_ffn_body, final=final),
        grid=(T // tm,),
        in_specs=[row(D_MODEL)] + [row(256)] * 4 +
                 [_const_spec((D_MODEL, D_MODEL)), _const_spec((1, D_MODEL)), _const_spec((D_MODEL, D_FF)),
                  _const_spec((D_MODEL, D_FF)), _const_spec((D_FF, D_MODEL)), _const_spec((1, D_MODEL))],
        out_specs=row(D_MODEL),
        out_shape=jax.ShapeDtypeStruct((T, D_MODEL), F32),
        input_output_aliases={0: 0} if in_place else {},
        compiler_params=_cparams(("parallel",)),
        name="out_ffn",
    )(x, *mixes, wo, g, wg, wu, wd, fg)


def _channel_dft():
    c = np.arange(GROUP_W)
    same = (c[:, None] // FN_CH) == (c[None, :] // FN_CH)
    ang = 2.0 * np.pi * ((c[:, None] % FN_CH) * (c[None, :] % FN_CH) % FN_CH) / FN_CH
    cs = np.concatenate([np.where(same, np.cos(ang), 0.0), np.where(same, np.sin(ang), 0.0)], axis=1)
    return jnp.asarray(cs / math.sqrt(FN_CH), BF16)


def _seq_dft(L):
    split = 64
    l = jnp.arange(L, dtype=jnp.int32)[None, :]
    ka = jnp.arange(L // split, dtype=jnp.int32)[:, None] * split
    kb = jnp.arange(split, dtype=jnp.int32)[:, None]
    ang = lambda kk: (2.0 * math.pi / L) * lax.rem(kk * l, L).astype(F32)
    ca, sa, cb, sb = jnp.cos(ang(ka)), jnp.sin(ang(ka)), jnp.cos(ang(kb)), jnp.sin(ang(kb))
    cc = ca[:, None, :] * cb[None, :, :] - sa[:, None, :] * sb[None, :, :]
    ss = sa[:, None, :] * cb[None, :, :] + ca[:, None, :] * sb[None, :, :]
    tab = jnp.concatenate([cc.reshape(L, L), -ss.reshape(L, L)], axis=1) * (L ** -0.5)
    return tab.astype(BF16)


def _rope_tables(max_len):
    half = DA_QDIM // 2
    pos = jnp.arange(max_len, dtype=F32)
    inv = jnp.power(ROPE_THETA, -jnp.arange(half, dtype=F32) * (2.0 / DA_QDIM))
    ang = pos[:, None] * inv[None, :]
    cos, sin = jnp.cos(ang), jnp.sin(ang)
    zero = jnp.zeros_like(sin)
    reps = 128 // DA_QDIM
    return (jnp.tile(jnp.concatenate([cos, cos], axis=1), (1, reps)),
            jnp.tile(jnp.concatenate([-sin, zero], axis=1), (1, reps)),
            jnp.tile(jnp.concatenate([zero, sin], axis=1), (1, reps)))


def _prepare(p):
    row2 = lambda v: v.reshape(1, -1).astype(F32)
    pad128 = lambda v: jnp.pad(v.reshape(1, -1).astype(F32), ((0, 0), (0, 128 - v.size)))
    c = np.arange(128)
    shared = dict(
        cdft=_channel_dft(),
        gmat=jnp.asarray(((c[:, None] // 64) == (c[None, :] // 64)) / 64.0, BF16),
        nmat=jnp.asarray((c[:, None] // DA_QDIM) == (c[None, :] // DA_QDIM), BF16),
        final_g=row2(p['final_norm_g']))
    layers = []
    for i in range(DEPTH):
        layers.append(dict(
            lambda_init=0.8 - 0.6 * math.exp(-0.3 * i),
            attn_g=row2(p['attn_norm_g'][i]),
            w_in=jnp.pad(p['w_in'][i], ((0, 0), (0, D_IN_PAD - D_IN))).astype(BF16),
            wf=p['w_fourier'][i].astype(BF16),
            g128=jnp.tile(row2(p['diff_subln_g'][i]), (1, 2)),
            lam_vecs=p['diff_lambda'][i].astype(F32),
            na_bias=_na_bias_table(p['na_rpb'][i]),
            conv_w=jnp.pad(p['ssm_conv_w'][i].astype(F32), ((0, 8 - SSM_CONV), (0, 0))),
            conv_b=row2(p['ssm_conv_b'][i]),
            dtb=pad128(p['ssm_dt_bias'][i]), alog=pad128(p['ssm_A_log'][i]),
            dskip=jnp.repeat(row2(p['ssm_D'][i]), GROUP_W // SSM_HEADS, axis=1),
            ssm_g=row2(p['ssm_norm_g'][i]),
            w_out=p['w_out'][i].astype(BF16), ffn_g=row2(p['ffn_norm_g'][i]),
            w_gate=p['w_gate'][i].astype(BF16), w_up=p['w_up'][i].astype(BF16),
            w_down=p['w_down'][i].astype(BF16)))
    return shared, layers


def _trunk(x, bsz, L, prep, tm_proj=512, tm_ffn=256, tm_dft=512, tq=256, tc=512):
    shared, layers = prep
    segs = ((0, bsz, L),)
    rope_tabs = _rope_tables(L)
    dft_tab = _seq_dft(L)
    for i, lp in enumerate(layers):
        (xcs, daq, dak, dav, naq, nak, nav, z, xbc, dt, nrm) = _in_proj(
            x, lp['attn_g'], lp['w_in'], shared['cdft'], shared['nmat'], rope_tabs, segs, min(tm_proj, L))
        o_fn = _fourier(xcs, dft_tab, lp['wf'], bsz, L, min(tm_dft, L))
        o_da = _diff_attn(daq, dak, dav, _da_noshift_flags(nrm, bsz), lp['lam_vecs'], lp['g128'], shared['gmat'],
                          bsz, L, min(tq, L), lp['lambda_init'])
        o_na = _nbr_attn(naq, nak, nav, lp['na_bias'], bsz, L)
        act = _conv_silu(xbc, lp['conv_w'], lp['conv_b'], segs, min(tc, L))
        o_ssm = _ssd_scan(act, z, dt, lp['dtb'], lp['alog'], lp['dskip'], lp['ssm_g'], bsz, L)
        x = _out_ffn(x, (o_fn, o_da, o_na, o_ssm), lp['w_out'], lp['ffn_g'], lp['w_gate'], lp['w_up'],
                     lp['w_down'], shared['final_g'], min(tm_ffn, L), final=(i == DEPTH - 1), in_place=(i > 0))
    return x


def kernel(x_prompt, x_sample, attn_norm_g, w_in, w_fourier, diff_lambda, diff_subln_g, na_rpb, ssm_conv_w,
           ssm_conv_b, ssm_dt_bias, ssm_A_log, ssm_D, ssm_norm_g, w_out, ffn_norm_g, w_gate, w_up, w_down,
           final_norm_g):
    prep = _prepare(dict(
        attn_norm_g=attn_norm_g, w_in=w_in, w_fourier=w_fourier, diff_lambda=diff_lambda,
        diff_subln_g=diff_subln_g, na_rpb=na_rpb, ssm_conv_w=ssm_conv_w, ssm_conv_b=ssm_conv_b,
        ssm_dt_bias=ssm_dt_bias, ssm_A_log=ssm_A_log, ssm_D=ssm_D, ssm_norm_g=ssm_norm_g,
        w_out=w_out, ffn_norm_g=ffn_norm_g, w_gate=w_gate, w_up=w_up, w_down=w_down,
        final_norm_g=final_norm_g))
    outs = []
    for xb in (x_prompt, x_sample):
        bsz, L, _ = xb.shape
        outs.append(_trunk(xb.reshape(bsz * L, D_MODEL), bsz, L, prep).reshape(xb.shape))
    return tuple(outs)
```
